```python
import math
import jax, jax.numpy as jnp
from jax import lax
import numpy as np

D_MODEL = 1024
BATCH = 2
SEQ = 8192
DEPTH = 1
DEC_BATCH = 32
DEC_SEQ = 8
PAST_LEN = 8192
PAGE_SIZE = 128

D_ATTN = D_MODEL // 2
D_SSM = D_MODEL - D_ATTN
H_A = 4
DV = D_ATTN // H_A
DQK = DV // 2
Q_BLOCK = 128
SSM_CH = 16
SSM_GROUPS = D_SSM // SSM_CH
SSM_P = 64
PEER_HEADS = 8
PEER_KEYS = 128
PEER_EXPERTS = PEER_KEYS * PEER_KEYS
PEER_TOPK = 16
PEER_DKEY = 256
PEER_DHALF = PEER_DKEY // 2
PEER_BLOCK = 128
ALPHA = (2.0 * DEPTH) ** 0.25
BETA = (8.0 * DEPTH) ** -0.25
EPS = 1e-5
MASK_VALUE = -1e30

kernel_name = "hymba_diffattn_s5_peer_step"


def layer_norm(x, g, b):
    xf = x.astype(jnp.float32)
    mu = jnp.mean(xf, axis=-1, keepdims=True)
    var = jnp.mean(jnp.square(xf - mu), axis=-1, keepdims=True)
    return ((xf - mu) * lax.rsqrt(var + EPS) * g.astype(jnp.float32) + b.astype(jnp.float32)).astype(x.dtype)


def rms_norm(x, g):
    xf = x.astype(jnp.float32)
    return (xf * lax.rsqrt(jnp.mean(xf * xf, axis=-1, keepdims=True) + EPS) * g.astype(jnp.float32)).astype(x.dtype)


def alibi_slopes():
    return 2.0 ** (-8.0 * jnp.arange(1, H_A + 1, dtype=jnp.float32) / H_A)


def project(x, w_in):
    b, t = x.shape[:2]
    z = x @ w_in
    q = z[..., :D_ATTN].reshape(b, t, H_A, 2, DQK)
    k = z[..., D_ATTN:2 * D_ATTN].reshape(b, t, H_A, 2, DQK)
    v = z[..., 2 * D_ATTN:3 * D_ATTN].reshape(b, t, H_A, DV)
    u = z[..., 3 * D_ATTN:]
    return q, k, v, u


def diff_attention(q, k, v, q_pos, k_pos, lam, subln_g, lam_init):
    s = jnp.einsum('bqhcd,bkhcd->bhcqk', q, k).astype(jnp.float32) * (DQK ** -0.5)
    dist = (q_pos[:, None] - k_pos[None, :]).astype(jnp.float32)
    bias = -alibi_slopes()[:, None, None] * jnp.abs(dist)
    causal = q_pos[:, None] >= k_pos[None, :]
    s = jnp.where(causal, s + bias[None, :, None], MASK_VALUE)
    p = jax.nn.softmax(s, axis=-1)
    a = p[:, :, 0] - lam * p[:, :, 1]
    o = jnp.einsum('bhqk,bkhd->bqhd', a.astype(v.dtype), v)
    o = rms_norm(o, subln_g) * (1.0 - lam_init)
    return o.reshape(o.shape[0], o.shape[1], D_ATTN)


def prompt_attention(q, k, v, lam, subln_g, lam_init):
    b, s = q.shape[:2]
    nb = s // Q_BLOCK
    qb = q.reshape(b, nb, Q_BLOCK, H_A, 2, DQK).transpose(1, 0, 2, 3, 4, 5)
    k_pos = jnp.arange(s, dtype=jnp.int32)

    def one_block(args):
        q_blk, blk = args
        q_pos = blk * Q_BLOCK + jnp.arange(Q_BLOCK, dtype=jnp.int32)
        return diff_attention(q_blk, k, v, q_pos, k_pos, lam, subln_g, lam_init)

    o = lax.map(one_block, (qb, jnp.arange(nb, dtype=jnp.int32)))
    return o.transpose(1, 0, 2, 3).reshape(b, s, D_ATTN)


def ssm_combine(e1, e2):
    ar1, ai1, br1, bi1 = e1
    ar2, ai2, br2, bi2 = e2
    return (ar2 * ar1 - ai2 * ai1, ar2 * ai1 + ai2 * ar1,
            ar2 * br1 - ai2 * bi1 + br2, ar2 * bi1 + ai2 * br1 + bi2)


def s5_mixer(u, h0_re, h0_im, lam_re, lam_im, log_dt, b_re, b_im, c_re, c_im, d_skip, w_glu, b_glu):
    f32 = jnp.float32
    bsz, t = u.shape[:2]
    uf = u.astype(f32).reshape(bsz, t, SSM_GROUPS, SSM_CH)
    lr, li = lam_re.astype(f32), lam_im.astype(f32)
    dt = jnp.exp(log_dt.astype(f32))[:, None]
    mag = jnp.exp(lr * dt)
    a_re, a_im = mag * jnp.cos(li * dt), mag * jnp.sin(li * dt)
    den = lr * lr + li * li
    n_re, n_im = a_re - 1.0, a_im
    g_re = (n_re * lr + n_im * li) / den
    g_im = (n_im * lr - n_re * li) / den
    br, bi = b_re.astype(f32), b_im.astype(f32)
    bb_re = g_re[..., None] * br - g_im[..., None] * bi
    bb_im = g_re[..., None] * bi + g_im[..., None] * br
    bu_re = jnp.einsum('btgc,gpc->btgp', uf, bb_re)
    bu_im = jnp.einsum('btgc,gpc->btgp', uf, bb_im)
    h0r, h0i = h0_re.astype(f32), h0_im.astype(f32)
    bu_re = bu_re.at[:, 0].add(a_re * h0r - a_im * h0i)
    bu_im = bu_im.at[:, 0].add(a_re * h0i + a_im * h0r)
    ar = jnp.broadcast_to(a_re, bu_re.shape)
    ai = jnp.broadcast_to(a_im, bu_im.shape)
    _, _, h_re, h_im = lax.associative_scan(ssm_combine, (ar, ai, bu_re, bu_im), axis=1)
    y = (jnp.einsum('gcp,btgp->btgc', c_re.astype(f32), h_re)
         - jnp.einsum('gcp,btgp->btgc', c_im.astype(f32), h_im)
         + d_skip.astype(f32) * uf)
    y = jax.nn.gelu(y.reshape(bsz, t, D_SSM), approximate=False)
    out = y * jax.nn.sigmoid(y @ w_glu.astype(f32) + b_glu.astype(f32))
    return out.astype(u.dtype), h_re[:, -1], h_im[:, -1]


def peer_block(x, w_q, q_g, sub_keys, u_tab, v_tab):
    t = x.shape[0]
    q = rms_norm((x @ w_q).reshape(t, PEER_HEADS, PEER_DKEY), q_g).reshape(t, PEER_HEADS, 2, PEER_DHALF)
    s = jnp.einsum('thcd,hcnd->thcn', q, sub_keys).astype(jnp.float32)
    s_top, i_top = lax.top_k(s, PEER_TOPK)
    cand = (s_top[:, :, 0, :, None] + s_top[:, :, 1, None, :]).reshape(t, PEER_HEADS, PEER_TOPK * PEER_TOPK)
    cand_idx = (i_top[:, :, 0, :, None] * PEER_KEYS + i_top[:, :, 1, None, :]).reshape(t, PEER_HEADS, PEER_TOPK * PEER_TOPK)
    best, pos = lax.top_k(cand, PEER_TOPK)
    idx = jnp.take_along_axis(cand_idx, pos, axis=-1)
    g = jax.nn.softmax(best, axis=-1)
    u = jnp.take(u_tab, idx, axis=0)
    act = jax.nn.gelu(jnp.einsum('thkd,td->thk', u, x).astype(jnp.float32), approximate=False)
    v = jnp.take(v_tab, idx, axis=0)
    return jnp.einsum('thk,thkd->td', (g * act).astype(x.dtype), v)


def peer_apply(x, w_q, q_g, sub_keys, u_tab, v_tab):
    b, t, d = x.shape
    n = b * t
    blk = PEER_BLOCK if n % PEER_BLOCK == 0 else n
    xb = x.reshape(n // blk, blk, d)
    out = lax.map(lambda xi: peer_block(xi, w_q, q_g, sub_keys, u_tab, v_tab), xb)
    return out.reshape(b, t, d)


def finish(x, a, s, w_out, ln1_g, ln1_b, peer_w_q, peer_q_g, peer_sub_keys, peer_u, peer_v, ln2_g, ln2_b):
    h = jnp.concatenate([a, s], axis=-1) @ w_out
    x1 = layer_norm(ALPHA * x + h, ln1_g, ln1_b)
    f = peer_apply(x1, peer_w_q, peer_q_g, peer_sub_keys, peer_u, peer_v)
    return layer_norm(ALPHA * x1 + f, ln2_g, ln2_b)


def setup_inputs(seed: int = 0) -> dict:
    key = jax.random.key(seed)
    ks = jax.random.split(key, 40)
    f32 = jnp.float32

    def nrm(k, shape, scale):
        return jax.random.normal(k, shape, f32) * scale

    n_pages = PAST_LEN // PAGE_SIZE
    n_phys = (DEC_BATCH * n_pages * 5) // 4
    page_table = jax.random.permutation(ks[0], n_phys)[:DEC_BATCH * n_pages].reshape(DEC_BATCH, n_pages).astype(jnp.int32)
    ws = D_MODEL ** -0.5
    w_in = jnp.concatenate([
        nrm(ks[7], (DEPTH, D_MODEL, 2 * D_ATTN), ws),
        nrm(ks[8], (DEPTH, D_MODEL, D_ATTN), ws * BETA),
        nrm(ks[9], (DEPTH, D_MODEL, D_SSM), ws),
    ], axis=-1)
    lam_im0 = jnp.broadcast_to(jnp.pi * jnp.arange(SSM_P, dtype=f32), (DEPTH, SSM_GROUPS, SSM_P))
    return {
        "x_prompt": nrm(ks[1], (BATCH, SEQ, D_MODEL), 1.0),
        "x_sample": nrm(ks[2], (DEC_BATCH, DEC_SEQ, D_MODEL), 1.0),
        "cache_k": nrm(ks[3], (DEPTH, n_phys, PAGE_SIZE, H_A, 2, DQK), 1.0),
        "cache_v": nrm(ks[4], (DEPTH, n_phys, PAGE_SIZE, H_A, DV), BETA),
        "state_ssm_re": nrm(ks[5], (DEPTH, DEC_BATCH, SSM_GROUPS, SSM_P), 0.1),
        "state_ssm_im": nrm(ks[6], (DEPTH, DEC_BATCH, SSM_GROUPS, SSM_P), 0.1),
        "page_table": page_table,
        "w_in": w_in,
        "lambda_q1": nrm(ks[10], (DEPTH, DQK), 0.1),
        "lambda_k1": nrm(ks[11], (DEPTH, DQK), 0.1),
        "lambda_q2": nrm(ks[12], (DEPTH, DQK), 0.1),
        "lambda_k2": nrm(ks[13], (DEPTH, DQK), 0.1),
        "attn_subln_g": 1.0 + nrm(ks[14], (DEPTH, DV), 0.01),
        "ssm_lambda_re": -0.5 + nrm(ks[15], (DEPTH, SSM_GROUPS, SSM_P), 1e-3),
        "ssm_lambda_im": lam_im0 + nrm(ks[16], (DEPTH, SSM_GROUPS, SSM_P), 1e-3),
        "ssm_log_dt": jax.random.uniform(ks[17], (DEPTH, SSM_GROUPS), f32, math.log(1e-3), math.log(1e-1)),
        "ssm_b_re": nrm(ks[18], (DEPTH, SSM_GROUPS, SSM_P, SSM_CH), (2 * SSM_CH) ** -0.5),
        "ssm_b_im": nrm(ks[19], (DEPTH, SSM_GROUPS, SSM_P, SSM_CH), (2 * SSM_CH) ** -0.5),
        "ssm_c_re": nrm(ks[20], (DEPTH, SSM_GROUPS, SSM_CH, SSM_P), (2 * SSM_P) ** -0.5),
        "ssm_c_im": nrm(ks[21], (DEPTH, SSM_GROUPS, SSM_CH, SSM_P), (2 * SSM_P) ** -0.5),
        "ssm_d": nrm(ks[22], (DEPTH, SSM_GROUPS, SSM_CH), 1.0),
        "ssm_w_glu": nrm(ks[23], (DEPTH, D_SSM, D_SSM), D_SSM ** -0.5),
        "ssm_b_glu": nrm(ks[24], (DEPTH, D_SSM), 0.01),
        "w_out": nrm(ks[25], (DEPTH, D_MODEL, D_MODEL), ws * BETA),
        "ln1_g": 1.0 + nrm(ks[26], (DEPTH, D_MODEL), 0.01),
        "ln1_b": nrm(ks[27], (DEPTH, D_MODEL), 0.01),
        "peer_w_q": nrm(ks[28], (DEPTH, D_MODEL, PEER_HEADS * PEER_DKEY), ws),
        "peer_q_g": 1.0 + nrm(ks[29], (DEPTH, PEER_DKEY), 0.01),
        "peer_sub_keys": nrm(ks[30], (DEPTH, PEER_HEADS, 2, PEER_KEYS, PEER_DHALF), PEER_DHALF ** -0.5),
        "peer_u": nrm(ks[31], (DEPTH, PEER_EXPERTS, D_MODEL), ws),
        "peer_v": nrm(ks[32], (DEPTH, PEER_EXPERTS, D_MODEL), BETA),
        "ln2_g": 1.0 + nrm(ks[33], (DEPTH, D_MODEL), 0.01),
        "ln2_b": nrm(ks[34], (DEPTH, D_MODEL), 0.01),
    }


def reference(x_prompt, x_sample, cache_k, cache_v, state_ssm_re, state_ssm_im, page_table,
              w_in, lambda_q1, lambda_k1, lambda_q2, lambda_k2, attn_subln_g,
              ssm_lambda_re, ssm_lambda_im, ssm_log_dt, ssm_b_re, ssm_b_im, ssm_c_re, ssm_c_im,
              ssm_d, ssm_w_glu, ssm_b_glu, w_out, ln1_g, ln1_b,
              peer_w_q, peer_q_g, peer_sub_keys, peer_u, peer_v, ln2_g, ln2_b):
    f32 = jnp.float32
    n_pages = PAST_LEN // PAGE_SIZE
    sample_q_pos = PAST_LEN + jnp.arange(DEC_SEQ, dtype=jnp.int32)
    sample_k_pos = jnp.arange(PAST_LEN + DEC_SEQ, dtype=jnp.int32)
    zero_state = jnp.zeros((BATCH, SSM_GROUPS, SSM_P), f32)
    xp, xs = x_prompt, x_sample
    kp_rows, vp_rows, hpr, hpi = [], [], [], []
    ks_rows, vs_rows, hsr, hsi = [], [], [], []
    for l in range(DEPTH):
        lam_init = 0.8 - 0.6 * math.exp(-0.3 * l)
        lam = (jnp.exp(jnp.sum(lambda_q1[l].astype(f32) * lambda_k1[l].astype(f32)))
               - jnp.exp(jnp.sum(lambda_q2[l].astype(f32) * lambda_k2[l].astype(f32))) + lam_init)
        ssm_args = (ssm_lambda_re[l], ssm_lambda_im[l], ssm_log_dt[l], ssm_b_re[l], ssm_b_im[l],
                    ssm_c_re[l], ssm_c_im[l], ssm_d[l], ssm_w_glu[l], ssm_b_glu[l])
        tail_args = (w_out[l], ln1_g[l], ln1_b[l], peer_w_q[l], peer_q_g[l], peer_sub_keys[l],
                     peer_u[l], peer_v[l], ln2_g[l], ln2_b[l])

        qp, kp, vp, up = project(xp, w_in[l])
        ap = prompt_attention(qp, kp, vp, lam, attn_subln_g[l], lam_init)
        sp, h_re_p, h_im_p = s5_mixer(up, zero_state, zero_state, *ssm_args)
        xp = finish(xp, ap, sp, *tail_args)

        qs, kss, vss, us = project(xs, w_in[l])
        k_past = cache_k[l][page_table].reshape(DEC_BATCH, n_pages * PAGE_SIZE, H_A, 2, DQK)
        v_past = cache_v[l][page_table].reshape(DEC_BATCH, n_pages * PAGE_SIZE, H_A, DV)
        k_all = jnp.concatenate([k_past.astype(kss.dtype), kss], axis=1)
        v_all = jnp.concatenate([v_past.astype(vss.dtype), vss], axis=1)
        a_s = diff_attention(qs, k_all, v_all, sample_q_pos, sample_k_pos, lam, attn_subln_g[l], lam_init)
        s_s, h_re_s, h_im_s = s5_mixer(us, state_ssm_re[l], state_ssm_im[l], *ssm_args)
        xs = finish(xs, a_s, s_s, *tail_args)

        kp_rows.append(kp); vp_rows.append(vp); hpr.append(h_re_p); hpi.append(h_im_p)
        ks_rows.append(kss); vs_rows.append(vss); hsr.append(h_re_s); hsi.append(h_im_s)

    k_prompt = jnp.stack(kp_rows)
    v_prompt = jnp.stack(vp_rows)
    ssm_re_prompt = jnp.stack(hpr)
    ssm_im_prompt = jnp.stack(hpi)
    k_sample = jnp.stack(ks_rows)
    v_sample = jnp.stack(vs_rows)
    ssm_re_sample = jnp.stack(hsr)
    ssm_im_sample = jnp.stack(hsi)
    return (xp, xs, k_prompt, v_prompt, ssm_re_prompt, ssm_im_prompt,
            k_sample, v_sample, ssm_re_sample, ssm_im_sample)
```

```python
import functools
import math

import jax
import jax.numpy as jnp
from jax import lax
from jax.experimental import pallas as pl
from jax.experimental.pallas import tpu as pltpu

F32 = jnp.float32
BF16 = jnp.bfloat16

EPS = 1e-5
MASK_VALUE = -1e30
PEER_TOPK = 16
NEG_INF = float("-inf")

TOKEN_PAD = 512
PROJ_TM = 512
ATTN_TQ = 512
SSM_CHUNK = 16
SEL_TB = 256
PEER_TB = 512
PEER_EC = 1024
VMEM_LIMIT = 56 * 1024 * 1024

_NT = (((1,), (1,)), ((), ()))
_HI = lax.Precision.HIGHEST


def _params(*sem):
    return pltpu.CompilerParams(dimension_semantics=sem, vmem_limit_bytes=VMEM_LIMIT)


def _gelu(x):
    return 0.5 * x * (1.0 + lax.erf(x * (2.0 ** -0.5)))


def _proj_kernel(x_ref, w_ref, k_ref, v_ref, u_ref, qkv_ref, *, d_attn, scale):
    z = jnp.dot(x_ref[...].astype(BF16), w_ref[...], preferred_element_type=F32)
    k = z[:, d_attn:2 * d_attn]
    v = z[:, 2 * d_attn:3 * d_attn]
    k_ref[...] = k
    v_ref[...] = v
    u_ref[...] = z[:, 3 * d_attn:]
    qkv_ref[:, :d_attn] = (z[:, :d_attn] * scale).astype(BF16)
    qkv_ref[:, d_attn:2 * d_attn] = k.astype(BF16)
    qkv_ref[:, 2 * d_attn:] = v.astype(BF16)


def _project(x_all, w_bf, d_attn, scale):
    tp, d_model = x_all.shape
    n_out = w_bf.shape[1]
    d_u = n_out - 3 * d_attn
    tm = PROJ_TM
    return pl.pallas_call(
        functools.partial(_proj_kernel, d_attn=d_attn, scale=scale),
        grid=(tp // tm,),
        in_specs=[pl.BlockSpec((tm, d_model), lambda i: (i, 0)),
                  pl.BlockSpec((d_model, n_out), lambda i: (0, 0))],
        out_specs=[pl.BlockSpec((tm, d_attn), lambda i: (i, 0)),
                   pl.BlockSpec((tm, d_attn), lambda i: (i, 0)),
                   pl.BlockSpec((tm, d_u), lambda i: (i, 0)),
                   pl.BlockSpec((tm, 3 * d_attn), lambda i: (i, 0))],
        out_shape=[jax.ShapeDtypeStruct((tp, d_attn), F32),
                   jax.ShapeDtypeStruct((tp, d_attn), F32),
                   jax.ShapeDtypeStruct((tp, d_u), F32),
                   jax.ShapeDtypeStruct((tp, 3 * d_attn), BF16)],
        compiler_params=_params("arbitrary"),
        name="proj",
    )(x_all, w_bf)


def _lam_value(lamv_ref, lam_init):
    lv = lamv_ref[...]
    e1 = jnp.exp(jnp.sum(lv[0:1] * lv[1:2], axis=1, keepdims=True))
    e2 = jnp.exp(jnp.sum(lv[2:3] * lv[3:4], axis=1, keepdims=True))
    return e1 - e2 + lam_init


def _subln(o, g, lam_init):
    ms = jnp.mean(o * o, axis=-1, keepdims=True)
    return o * lax.rsqrt(ms + EPS) * g * (1.0 - lam_init)


def _attn_prompt_kernel(lamv_ref, g_ref, slope_ref, q_ref, k_ref, v_ref, o_ref,
                        m_ref, l_ref, acc_ref, *, tq, dqk, lam_init):
    i = pl.program_id(2)
    j = pl.program_id(3)

    @pl.when(j == 0)
    def _():
        m_ref[...] = jnp.full(m_ref.shape, MASK_VALUE, F32)
        l_ref[...] = jnp.zeros(l_ref.shape, F32)
        acc_ref[...] = jnp.zeros(acc_ref.shape, F32)

    def step(diagonal):
        q = q_ref[...]
        k = k_ref[...]
        v = v_ref[...]
        lane = lax.broadcasted_iota(jnp.int32, q.shape, 1)
        zero = jnp.zeros_like(q)
        qs = (jnp.where(lane < dqk, q, zero), jnp.where(lane < dqk, zero, q))
        kcol = lax.broadcasted_iota(jnp.int32, (1, tq), 1) + (j - i) * tq
        colbias = slope_ref[...][:, :1] * kcol.astype(F32)
        if diagonal:
            row = lax.broadcasted_iota(jnp.int32, (tq, tq), 0)
            col = lax.broadcasted_iota(jnp.int32, (tq, tq), 1)
            keep = row >= col
        for c in range(2):
            s = lax.dot_general(qs[c], k, _NT, preferred_element_type=F32) + colbias
            if diagonal:
                s = jnp.where(keep, s, MASK_VALUE)
            m_prev = m_ref[c]
            m_new = jnp.maximum(m_prev, jnp.max(s, axis=1, keepdims=True))
            alpha = jnp.exp(m_prev - m_new)
            p = jnp.exp(s - m_new)
            l_ref[c] = alpha * l_ref[c] + jnp.sum(p, axis=1, keepdims=True)
            acc_ref[c] = alpha * acc_ref[c] + jnp.dot(p.astype(BF16), v, preferred_element_type=F32)
            m_ref[c] = m_new

    @pl.when(j < i)
    def _():
        step(False)

    @pl.when(j == i)
    def _():
        step(True)
        lam = _lam_value(lamv_ref, lam_init)
        o = acc_ref[0] / l_ref[0] - lam * (acc_ref[1] / l_ref[1])
        o_ref[...] = _subln(o, g_ref[...], lam_init).astype(o_ref.dtype)


def _attn_prompt(qkv_bf, lamv, subln_g, slopes, batch, seq, n_heads, dv, dqk, lam_init):
    tq = ATTN_TQ
    nq = seq // tq
    d_attn = n_heads * dv
    return pl.pallas_call(
        functools.partial(_attn_prompt_kernel, tq=tq, dqk=dqk, lam_init=lam_init),
        grid=(batch, n_heads, nq, nq),
        in_specs=[pl.BlockSpec(lamv.shape, lambda b, h, i, j: (0, 0)),
                  pl.BlockSpec((1, dv), lambda b, h, i, j: (0, 0)),
                  pl.BlockSpec((None, 1, 128), lambda b, h, i, j: (h, 0, 0)),
                  pl.BlockSpec((tq, dv), lambda b, h, i, j: (b * nq + i, h)),
                  pl.BlockSpec((tq, dv), lambda b, h, i, j: (b * nq + jnp.minimum(i, j), n_heads + h)),
                  pl.BlockSpec((tq, dv), lambda b, h, i, j: (b * nq + jnp.minimum(i, j), 2 * n_heads + h))],
        out_specs=pl.BlockSpec((tq, dv), lambda b, h, i, j: (b * nq + i, h)),
        out_shape=jax.ShapeDtypeStruct((batch * seq, d_attn), BF16),
        scratch_shapes=[pltpu.VMEM((2, tq, 1), F32), pltpu.VMEM((2, tq, 1), F32),
                        pltpu.VMEM((2, tq, dv), F32)],
        compiler_params=_params("arbitrary", "arbitrary", "arbitrary", "arbitrary"),
        name="attn_prompt",
    )(lamv, subln_g, slopes, qkv_bf, qkv_bf, qkv_bf)


def _attn_sample_kernel(pt_ref, lamv_ref, g_ref, slopecol_ref, qbd_ref, kc_ref, vc_ref,
                        kn_ref, vn_ref, o_ref, m_ref, l_ref, acc_ref,
                        *, n_pages, page, past_len, n_heads, dv, n_new, lam_init):
    del pt_ref
    p_id = pl.program_id(1)
    rows_h = 2 * n_new

    @pl.when(p_id == 0)
    def _():
        m_ref[...] = jnp.full(m_ref.shape, MASK_VALUE, F32)
        l_ref[...] = jnp.zeros(l_ref.shape, F32)
        acc_ref[...] = jnp.zeros(acc_ref.shape, F32)

    def update(s, v):
        m_prev = m_ref[...]
        m_new = jnp.maximum(m_prev, jnp.max(s, axis=1, keepdims=True))
        alpha = jnp.exp(m_prev - m_new)
        p = jnp.exp(s - m_new)
        l_ref[...] = alpha * l_ref[...] + jnp.sum(p, axis=1, keepdims=True)
        pb = p.astype(BF16)
        for h in range(n_heads):
            r = slice(h * rows_h, (h + 1) * rows_h)
            acc_ref[r, :] = alpha[r] * acc_ref[r, :] + jnp.dot(
                pb[r], v[:, h * dv:(h + 1) * dv], preferred_element_type=F32)
        m_ref[...] = m_new

    @pl.when(p_id < n_pages)
    def _():
        k = kc_ref[...].astype(BF16)
        v = vc_ref[...].astype(BF16)
        s = lax.dot_general(qbd_ref[...], k, _NT, preferred_element_type=F32)
        kpos = lax.broadcasted_iota(jnp.int32, (1, page), 1) + (p_id * page - past_len)
        update(s + slopecol_ref[...] * kpos.astype(F32), v)

    @pl.when(p_id == n_pages)
    def _():
        k = kn_ref[...]
        v = vn_ref[...]
        nk = k.shape[0]
        s = lax.dot_general(qbd_ref[...], k, _NT, preferred_element_type=F32)
        col = lax.broadcasted_iota(jnp.int32, (s.shape[0], nk), 1)
        qi = lax.broadcasted_iota(jnp.int32, (s.shape[0], nk), 0) & (n_new - 1)
        s = s + slopecol_ref[...] * col.astype(F32)
        update(jnp.where(col <= qi, s, MASK_VALUE), v)
        lam = _lam_value(lamv_ref, lam_init)
        a = acc_ref[...] / l_ref[...]
        for h in range(n_heads):
            o = a[h * rows_h:h * rows_h + n_new] - lam * a[h * rows_h + n_new:(h + 1) * rows_h]
            o_ref[h * n_new:(h + 1) * n_new, :] = _subln(o, g_ref[...], lam_init).astype(o_ref.dtype)


def _attn_sample(page_table, lamv, subln_g, slopecol, qbd, cache_k, cache_v, k_new, v_new,
                 n_heads, dv, n_new, lam_init):
    dec_batch, n_pages = page_table.shape
    _, page, d_attn = cache_k.shape
    n_rows = qbd.shape[1]
    nk = k_new.shape[1]
    pt_flat = page_table.reshape(-1)

    def cache_map(b, p, pt):
        return (pt[b * n_pages + jnp.minimum(p, n_pages - 1)], 0, 0)

    kernel = functools.partial(
        _attn_sample_kernel, n_pages=n_pages, page=page, past_len=n_pages * page,
        n_heads=n_heads, dv=dv, n_new=n_new, lam_init=lam_init)
    grid_spec = pltpu.PrefetchScalarGridSpec(
        num_scalar_prefetch=1,
        grid=(dec_batch, n_pages + 1),
        in_specs=[pl.BlockSpec(lamv.shape, lambda b, p, pt: (0, 0)),
                  pl.BlockSpec((1, dv), lambda b, p, pt: (0, 0)),
                  pl.BlockSpec((n_rows, 1), lambda b, p, pt: (0, 0)),
                  pl.BlockSpec((None, n_rows, d_attn), lambda b, p, pt: (b, 0, 0)),
                  pl.BlockSpec((None, page, d_attn), cache_map),
                  pl.BlockSpec((None, page, d_attn), cache_map),
                  pl.BlockSpec((None, nk, d_attn), lambda b, p, pt: (b, 0, 0)),
                  pl.BlockSpec((None, nk, d_attn), lambda b, p, pt: (b, 0, 0))],
        out_specs=pl.BlockSpec((None, n_heads * n_new, dv), lambda b, p, pt: (b, 0, 0)),
        scratch_shapes=[pltpu.VMEM((n_rows, 1), F32), pltpu.VMEM((n_rows, 1), F32),
                        pltpu.VMEM((n_rows, dv), F32)],
    )
    return pl.pallas_call(
        kernel,
        grid_spec=grid_spec,
        out_shape=jax.ShapeDtypeStruct((dec_batch, n_heads * n_new, dv), BF16),
        compiler_params=_params("arbitrary", "arbitrary"),
        name="attn_sample",
    )(pt_flat, lamv, subln_g, slopecol, qbd, cache_k, cache_v, k_new, v_new)


def _ssm_prompt_kernel(u_ref, tm_ref, sm_ref, cm_ref, d_ref, coef_ref, y_ref, h_ref,
                       *, n_chunks, n_steps):
    u = u_ref[...]
    x = jnp.dot(u, sm_ref[...], precision=_HI, preferred_element_type=F32)
    half = x.shape[1] // 2
    x0, x1 = x[:, :half], x[:, half:]
    row = lax.broadcasted_iota(jnp.int32, x0.shape, 0)

    def shifted(a, d):
        return jnp.where(row >= d, pltpu.roll(a, d, 0), 0.0)

    y0, y1 = shifted(x0, 1), shifted(x1, 1)
    coef = coef_ref[...]
    for s in range(n_steps):
        d = 1 << s
        p = coef[2 * s:2 * s + 1]
        q = coef[2 * s + 1:2 * s + 2]
        s0, s1 = shifted(y0, d), shifted(y1, d)
        y0, y1 = y0 + p * s0 + q * s1, y1 + p * s1 - q * s0
    last = n_chunks - 1
    h_ref[...] = (coef[0:1] * y0[last:last + 1] + coef[1:2] * y1[last:last + 1]
                  + x0[last:last + 1])
    y = (jnp.dot(u, tm_ref[...], precision=_HI, preferred_element_type=F32)
         + jnp.dot(y0, cm_ref[...], precision=_HI, preferred_element_type=F32)
         + u * d_ref[...])
    y_ref[...] = _gelu(y)


def _ssm_prompt(u_r, tm, sm, cm, dsk, coef):
    batch, groups, n_chunks, width = u_r.shape
    n_steps = coef.shape[1] // 2
    state = cm.shape[1]
    return pl.pallas_call(
        functools.partial(_ssm_prompt_kernel, n_chunks=n_chunks, n_steps=n_steps),
        grid=(batch, groups),
        in_specs=[pl.BlockSpec((None, None, n_chunks, width), lambda b, g: (b, g, 0, 0)),
                  pl.BlockSpec((None, width, width), lambda b, g: (g, 0, 0)),
                  pl.BlockSpec((None, width, 2 * state), lambda b, g: (g, 0, 0)),
                  pl.BlockSpec((None, state, width), lambda b, g: (g, 0, 0)),
                  pl.BlockSpec((None, 1, width), lambda b, g: (g, 0, 0)),
                  pl.BlockSpec((None, 2 * n_steps, state), lambda b, g: (g, 0, 0))],
        out_specs=[pl.BlockSpec((None, None, n_chunks, width), lambda b, g: (b, g, 0, 0)),
                   pl.BlockSpec((None, None, 1, state), lambda b, g: (b, g, 0, 0))],
        out_shape=[jax.ShapeDtypeStruct((batch, groups, n_chunks, width), F32),
                   jax.ShapeDtypeStruct((batch, groups, 1, state), F32)],
        compiler_params=_params("arbitrary", "arbitrary"),
        name="ssm_prompt",
    )(u_r, tm, sm, cm, dsk, coef)


def _ssm_sample_kernel(u_ref, h0_ref, h0s_ref, tm_ref, sm_ref, cm_ref, d_ref, coef_ref,
                       y_ref, h_ref):
    u = u_ref[...]
    h0 = h0_ref[...]
    coef = coef_ref[...]
    x = jnp.dot(u, sm_ref[...], precision=_HI, preferred_element_type=F32)
    h_ref[...] = coef[0:1] * h0 + coef[1:2] * h0s_ref[...] + x
    y = (jnp.dot(u, tm_ref[...], precision=_HI, preferred_element_type=F32)
         + jnp.dot(h0, cm_ref[...], precision=_HI, preferred_element_type=F32)
         + u * d_ref[...])
    y_ref[...] = _gelu(y)


def _ssm_sample(u_r, h0, h0s, tm, sm, cm, dsk, coef):
    groups, seqs, width = u_r.shape
    state = h0.shape[2]
    return pl.pallas_call(
        _ssm_sample_kernel,
        grid=(groups,),
        in_specs=[pl.BlockSpec((None, seqs, width), lambda g: (g, 0, 0)),
                  pl.BlockSpec((None, seqs, state), lambda g: (g, 0, 0)),
                  pl.BlockSpec((None, seqs, state), lambda g: (g, 0, 0)),
                  pl.BlockSpec((None, width, width), lambda g: (g, 0, 0)),
                  pl.BlockSpec((None, width, state), lambda g: (g, 0, 0)),
                  pl.BlockSpec((None, state, width), lambda g: (g, 0, 0)),
                  pl.BlockSpec((None, 1, width), lambda g: (g, 0, 0)),
                  pl.BlockSpec((None, 2, state), lambda g: (g, 0, 0))],
        out_specs=[pl.BlockSpec((None, seqs, width), lambda g: (g, 0, 0)),
                   pl.BlockSpec((None, seqs, state), lambda g: (g, 0, 0))],
        out_shape=[jax.ShapeDtypeStruct((groups, seqs, width), F32),
                   jax.ShapeDtypeStruct((groups, seqs, state), F32)],
        compiler_params=_params("arbitrary"),
        name="ssm_sample",
    )(u_r, h0, h0s, tm, sm, cm, dsk, coef)


def _ssm_tables(lam_re, lam_im, log_dt, b_re, b_im, c_re, c_im, d_skip, chunk, n_steps):
    groups, p_dim, ch = b_re.shape
    lr, li = lam_re.astype(F32), lam_im.astype(F32)
    dt = jnp.exp(log_dt.astype(F32))[:, None]
    mag = jnp.exp(lr * dt)
    a_re, a_im = mag * jnp.cos(li * dt), mag * jnp.sin(li * dt)
    den = lr * lr + li * li
    n_re, n_im = a_re - 1.0, a_im
    g_re = (n_re * lr + n_im * li) / den
    g_im = (n_im * lr - n_re * li) / den
    br, bi = b_re.astype(F32), b_im.astype(F32)
    bb_re = g_re[..., None] * br - g_im[..., None] * bi
    bb_im = g_re[..., None] * bi + g_im[..., None] * br

    def power(n):
        n = n.astype(F32)[:, None, None]
        m = jnp.exp(lr * dt * n)
        return m * jnp.cos(li * dt * n), m * jnp.sin(li * dt * n)

    taus = jnp.arange(chunk + 1)
    pw_re, pw_im = power(taus)
    ab_re = pw_re[:chunk, :, :, None] * bb_re - pw_im[:chunk, :, :, None] * bb_im
    ab_im = pw_re[:chunk, :, :, None] * bb_im + pw_im[:chunk, :, :, None] * bb_re
    cr, ci = c_re.astype(F32), c_im.astype(F32)
    kern = (jnp.einsum('gcp,tgpd->gtcd', cr, ab_re, precision=_HI)
            - jnp.einsum('gcp,tgpd->gtcd', ci, ab_im, precision=_HI))
    s_idx = jnp.arange(chunk)[:, None]
    t_idx = jnp.arange(chunk)[None, :]
    lag = t_idx - s_idx
    tm = jnp.where((lag >= 0)[None, :, :, None, None], kern[:, jnp.maximum(lag, 0)], 0.0)
    tm = tm.transpose(0, 1, 4, 2, 3).reshape(groups, chunk * ch, chunk * ch)
    sm_re = ab_re[::-1].transpose(1, 0, 3, 2).reshape(groups, chunk * ch, p_dim)
    sm_im = ab_im[::-1].transpose(1, 0, 3, 2).reshape(groups, chunk * ch, p_dim)
    sm = jnp.concatenate([sm_re, sm_im, sm_im, sm_re], axis=-1)
    ca_re = cr[:, None] * pw_re[1:].transpose(1, 0, 2)[:, :, None, :] \
        - ci[:, None] * pw_im[1:].transpose(1, 0, 2)[:, :, None, :]
    ca_im = cr[:, None] * pw_im[1:].transpose(1, 0, 2)[:, :, None, :] \
        + ci[:, None] * pw_re[1:].transpose(1, 0, 2)[:, :, None, :]
    cm = jnp.concatenate([ca_re, -ca_im], axis=-1)
    cm = cm.transpose(0, 3, 1, 2).reshape(groups, 2 * p_dim, chunk * ch)
    dsk = jnp.tile(d_skip.astype(F32), (1, chunk)).reshape(groups, 1, chunk * ch)
    steps = chunk * (2 ** jnp.arange(n_steps))
    sp_re, sp_im = power(steps)
    p_rows = jnp.concatenate([sp_re, sp_re], axis=-1)
    q_rows = jnp.concatenate([-sp_im, sp_im], axis=-1)
    coef = jnp.stack([p_rows, q_rows], axis=1).reshape(2 * n_steps, groups, 2 * p_dim)
    return tm, sm, cm, dsk, coef.transpose(1, 0, 2)


def _layer_norm(y, g, b, axis):
    mu = jnp.mean(y, axis=axis, keepdims=True)
    var = jnp.mean(jnp.square(y - mu), axis=axis, keepdims=True)
    return (y - mu) * lax.rsqrt(var + EPS) * g + b


def _out_kernel(x_ref, a_ref, yg_ref, wglu_ref, bglu_ref, wo_ref, g_ref, b_ref, x1_ref,
                *, d_attn, alpha):
    yg = yg_ref[...]
    gate = jax.nn.sigmoid(
        jnp.dot(yg.astype(BF16), wglu_ref[...], preferred_element_type=F32) + bglu_ref[...])
    s = (yg * gate).astype(BF16)
    h = (jnp.dot(a_ref[...], wo_ref[:d_attn, :], preferred_element_type=F32)
         + jnp.dot(s, wo_ref[d_attn:, :], preferred_element_type=F32))
    x1_ref[...] = _layer_norm(alpha * x_ref[...] + h, g_ref[...], b_ref[...], -1)


def _out_proj(x_all, a_all, yg_all, wglu_bf, bglu, wo_bf, ln_g, ln_b, alpha):
    tp, d_model = x_all.shape
    d_attn = a_all.shape[1]
    d_ssm = yg_all.shape[1]
    tm = PROJ_TM
    row = lambda i: (i, 0)
    fixed = lambda i: (0, 0)
    return pl.pallas_call(
        functools.partial(_out_kernel, d_attn=d_attn, alpha=alpha),
        grid=(tp // tm,),
        in_specs=[pl.BlockSpec((tm, d_model), row), pl.BlockSpec((tm, d_attn), row),
                  pl.BlockSpec((tm, d_ssm), row), pl.BlockSpec((d_ssm, d_ssm), fixed),
                  pl.BlockSpec((1, d_ssm), fixed), pl.BlockSpec((d_model, d_model), fixed),
                  pl.BlockSpec((1, d_model), fixed), pl.BlockSpec((1, d_model), fixed)],
        out_specs=pl.BlockSpec((tm, d_model), row),
        out_shape=jax.ShapeDtypeStruct((tp, d_model), F32),
        compiler_params=_params("arbitrary"),
        name="out_proj",
    )(x_all, a_all, yg_all, wglu_bf, bglu, wo_bf, ln_g, ln_b)


def _top_k_rows(s, k):
    n = s.shape[0]
    iota = lax.broadcasted_iota(jnp.int32, s.shape, 0).astype(F32)
    rank = jnp.full(s.shape, float(k), F32)
    vals, idxs = [], []
    for r in range(k):
        mx = jnp.max(s, axis=0, keepdims=True)
        ix = jnp.min(jnp.where(s == mx, iota, float(n)), axis=0, keepdims=True)
        hit = iota == ix
        rank = jnp.where(hit, float(r), rank)
        s = jnp.where(hit, NEG_INF, s)
        vals.append(mx)
        idxs.append(ix)
    return vals, idxs, rank


def _staircase(k):
    return [(a, b) for a in range(k) for b in range(k) if (a + 1) * (b + 1) <= k]


def _peer_sel_kernel(x1_ref, wq_ref, qg_ref, keys_ref, n1_ref, e1_ref, r2_ref, e2_ref,
                     *, n_heads, d_key, topk):
    qp = jnp.dot(x1_ref[...].astype(BF16), wq_ref[...], preferred_element_type=F32)
    d_half = d_key // 2
    pairs = _staircase(topk)
    n_pairs = len(pairs)
    n_rows = -(-n_pairs // 8) * 8
    tb = qp.shape[0]
    for h in range(n_heads):
        qh = qp[:, h * d_key:(h + 1) * d_key]
        qn = qh * lax.rsqrt(jnp.mean(qh * qh, axis=-1, keepdims=True) + EPS) * qg_ref[...]
        qb = qn.astype(BF16)
        s1 = lax.dot_general(keys_ref[2 * h], qb[:, :d_half], _NT, preferred_element_type=F32)
        s2 = lax.dot_general(keys_ref[2 * h + 1], qb[:, d_half:], _NT, preferred_element_type=F32)
        t1, i1, _ = _top_k_rows(s1, topk)
        t2, _, r2 = _top_k_rows(s2, topk)
        cand = jnp.concatenate(
            [t1[a] + t2[b] for a, b in pairs]
            + [jnp.full((n_rows - n_pairs, tb), NEG_INF, F32)], axis=0)
        iota = lax.broadcasted_iota(jnp.int32, cand.shape, 0).astype(F32)
        work = cand
        z = jnp.zeros((1, tb), F32)
        m = t1[0] + t2[0]
        for _ in range(topk):
            mx = jnp.max(work, axis=0, keepdims=True)
            ix = jnp.min(jnp.where(work == mx, iota, float(n_rows)), axis=0, keepdims=True)
            work = jnp.where(iota == ix, NEG_INF, work)
            z = z + jnp.exp(mx - m)
        chosen = jnp.where((work == NEG_INF) & (iota < float(n_pairs)), 1.0, 0.0)
        key_iota = lax.broadcasted_iota(jnp.int32, s1.shape, 0).astype(F32)
        n1 = jnp.zeros(s1.shape, F32)
        off = 0
        for a in range(topk):
            cnt = sum(1 for pa, _ in pairs if pa == a)
            n_a = jnp.sum(chosen[off:off + cnt], axis=0, keepdims=True)
            off += cnt
            n1 = jnp.where(key_iota == i1[a], n_a, n1)
        n1_ref[h] = n1
        e1_ref[h] = jnp.exp(s1 - t1[0])
        r2_ref[h] = r2
        e2_ref[h] = jnp.exp(s2 - t2[0]) / z


def _peer_select(x1, wq_bf, q_g, keys_bf, n_heads, d_key, n_keys):
    tp, d_model = x1.shape
    tb = SEL_TB
    out_spec = pl.BlockSpec((n_heads, n_keys, tb), lambda i: (0, 0, i))
    out_shape = jax.ShapeDtypeStruct((n_heads, n_keys, tp), F32)
    return pl.pallas_call(
        functools.partial(_peer_sel_kernel, n_heads=n_heads, d_key=d_key, topk=PEER_TOPK),
        grid=(tp // tb,),
        in_specs=[pl.BlockSpec((tb, d_model), lambda i: (i, 0)),
                  pl.BlockSpec(wq_bf.shape, lambda i: (0, 0)),
                  pl.BlockSpec((1, d_key), lambda i: (0, 0)),
                  pl.BlockSpec(keys_bf.shape, lambda i: (0, 0, 0))],
        out_specs=[out_spec] * 4,
        out_shape=[out_shape] * 4,
        compiler_params=_params("arbitrary"),
        name="peer_select",
    )(x1, wq_bf, q_g, keys_bf)


def _peer_dense_kernel(xt_ref, xtb_ref, u_ref, vt_ref, n1_ref, e1_ref, r2_ref, e2_ref,
                       g_ref, b_ref, o_ref, acc_ref, wg_ref, *, n_heads, n_keys, alpha):
    c = pl.program_id(1)

    @pl.when(c == 0)
    def _():
        acc_ref[...] = jnp.zeros(acc_ref.shape, F32)

    act = jnp.dot(u_ref[...], xtb_ref[...], preferred_element_type=F32)
    n_i = act.shape[0] // n_keys
    for il in range(n_i):
        w = None
        for h in range(n_heads):
            sel = r2_ref[h] < n1_ref[h, il:il + 1, :]
            term = jnp.where(sel, e2_ref[h], 0.0) * e1_ref[h, il:il + 1, :]
            w = term if w is None else w + term
        rows = slice(il * n_keys, (il + 1) * n_keys)
        wg_ref[rows, :] = (w * _gelu(act[rows])).astype(BF16)
    acc_ref[...] += jnp.dot(vt_ref[...], wg_ref[...], preferred_element_type=F32)

    @pl.when(c == pl.num_programs(1) - 1)
    def _():
        o_ref[...] = _layer_norm(alpha * xt_ref[...] + acc_ref[...], g_ref[...], b_ref[...], 0)


def _peer_dense(x1t, x1t_bf, u_bf, vt_bf, n1, e1, r2, e2, ln_g, ln_b, alpha):
    d_model, tp = x1t.shape
    n_heads, n_keys, _ = n1.shape
    n_exp = u_bf.shape[0]
    tb, ec = PEER_TB, PEER_EC
    n_i = ec // n_keys
    tok = lambda t, c: (0, t)
    return pl.pallas_call(
        functools.partial(_peer_dense_kernel, n_heads=n_heads, n_keys=n_keys, alpha=alpha),
        grid=(tp // tb, n_exp // ec),
        in_specs=[pl.BlockSpec((d_model, tb), tok),
                  pl.BlockSpec((d_model, tb), tok),
                  pl.BlockSpec((ec, d_model), lambda t, c: (c, 0)),
                  pl.BlockSpec((d_model, ec), lambda t, c: (0, c)),
                  pl.BlockSpec((n_heads, n_i, tb), lambda t, c: (0, c, t)),
                  pl.BlockSpec((n_heads, n_i, tb), lambda t, c: (0, c, t)),
                  pl.BlockSpec((n_heads, n_keys, tb), lambda t, c: (0, 0, t)),
                  pl.BlockSpec((n_heads, n_keys, tb), lambda t, c: (0, 0, t)),
                  pl.BlockSpec((d_model, 1), lambda t, c: (0, 0)),
                  pl.BlockSpec((d_model, 1), lambda t, c: (0, 0))],
        out_specs=pl.BlockSpec((d_model, tb), tok),
        out_shape=jax.ShapeDtypeStruct((d_model, tp), F32),
        scratch_shapes=[pltpu.VMEM((d_model, tb), F32), pltpu.VMEM((ec, tb), BF16)],
        compiler_params=_params("arbitrary", "arbitrary"),
        name="peer_dense",
    )(x1t, x1t_bf, u_bf, vt_bf, n1, e1, r2, e2, ln_g, ln_b)


def kernel(x_prompt, x_sample, cache_k, cache_v, state_ssm_re, state_ssm_im, page_table,
           w_in, lambda_q1, lambda_k1, lambda_q2, lambda_k2, attn_subln_g,
           ssm_lambda_re, ssm_lambda_im, ssm_log_dt, ssm_b_re, ssm_b_im, ssm_c_re, ssm_c_im,
           ssm_d, ssm_w_glu, ssm_b_glu, w_out, ln1_g, ln1_b,
           peer_w_q, peer_q_g, peer_sub_keys, peer_u, peer_v, ln2_g, ln2_b):
    depth = w_in.shape[0]
    batch, seq, d_model = x_prompt.shape
    dec_batch, dec_seq, _ = x_sample.shape
    _, n_phys, page, n_heads, _, dqk = cache_k.shape
    dv = cache_v.shape[-1]
    d_attn = n_heads * dv
    groups, p_dim = state_ssm_re.shape[2:]
    ch = ssm_b_re.shape[-1]
    peer_heads, _, n_keys, d_half = peer_sub_keys.shape[1:]
    d_key = 2 * d_half
    alpha = (2.0 * depth) ** 0.25
    scale = dqk ** -0.5

    n_prompt = batch * seq
    n_sample = dec_batch * dec_seq
    n_tok = n_prompt + n_sample
    tp = -(-n_tok // TOKEN_PAD) * TOKEN_PAD
    n_chunks = seq // SSM_CHUNK
    n_steps = int(math.log2(n_chunks))
    assert n_chunks == 1 << n_steps and seq % ATTN_TQ == 0 and n_prompt % TOKEN_PAD == 0

    x_all = jnp.concatenate(
        [x_prompt.reshape(n_prompt, d_model), x_sample.reshape(n_sample, d_model),
         jnp.zeros((tp - n_tok, d_model), x_prompt.dtype)], axis=0)
    slope_vals = 2.0 ** (-8.0 * jnp.arange(1, n_heads + 1, dtype=F32) / n_heads)
    slopes = jnp.broadcast_to(slope_vals[:, None, None], (n_heads, 1, 128))
    slopecol = jnp.repeat(slope_vals, 2 * dec_seq)[:, None]
    eye_hc = jnp.eye(2 * n_heads, dtype=F32)

    outs = {name: [] for name in ("kp", "vp", "hpr", "hpi", "ks", "vs", "hsr", "hsi")}
    for l in range(depth):
        lam_init = 0.8 - 0.6 * math.exp(-0.3 * l)
        lamv = jnp.stack([lambda_q1[l], lambda_k1[l], lambda_q2[l], lambda_k2[l]]).astype(F32)
        subln_g = attn_subln_g[l].astype(F32)[None, :]

        k_all, v_all, u_all, qkv_bf = _project(x_all, w_in[l].astype(BF16), d_attn, scale)

        a_prompt = _attn_prompt(qkv_bf, lamv, subln_g, slopes, batch, seq, n_heads, dv, dqk,
                                lam_init)

        qkv_s = qkv_bf[n_prompt:n_tok].reshape(dec_batch, dec_seq, 3 * d_attn)
        q_s = qkv_s[..., :d_attn].reshape(dec_batch, dec_seq, 2 * n_heads, dqk)
        qbd = (q_s.transpose(0, 2, 1, 3)[:, :, :, None, :].astype(F32)
               * eye_hc[None, :, None, :, None]).astype(BF16)
        qbd = qbd.reshape(dec_batch, 2 * n_heads * dec_seq, d_attn)
        pad_new = ((0, 0), (0, 128 - dec_seq), (0, 0))
        k_new = jnp.pad(qkv_s[..., d_attn:2 * d_attn], pad_new)
        v_new = jnp.pad(qkv_s[..., 2 * d_attn:], pad_new)
        a_s = _attn_sample(page_table, lamv, subln_g, slopecol, qbd,
                           cache_k[l].reshape(n_phys, page, d_attn),
                           cache_v[l].reshape(n_phys, page, d_attn),
                           k_new, v_new, n_heads, dv, dec_seq, lam_init)
        a_sample = a_s.reshape(dec_batch, n_heads, dec_seq, dv).transpose(0, 2, 1, 3)
        a_all = jnp.concatenate(
            [a_prompt, a_sample.reshape(n_sample, d_attn),
             jnp.zeros((tp - n_tok, d_attn), BF16)], axis=0)

        ssm_p = (ssm_lambda_re[l], ssm_lambda_im[l], ssm_log_dt[l], ssm_b_re[l], ssm_b_im[l],
                 ssm_c_re[l], ssm_c_im[l], ssm_d[l])
        tm, sm, cm, dsk, coef = _ssm_tables(*ssm_p, SSM_CHUNK, n_steps)
        u_r = (u_all[:n_prompt].reshape(batch, n_chunks, SSM_CHUNK, groups, ch)
               .transpose(0, 3, 1, 2, 4).reshape(batch, groups, n_chunks, SSM_CHUNK * ch))
        y_r, h_p = _ssm_prompt(u_r, tm, sm, cm, dsk, coef)
        yg_prompt = (y_r.reshape(batch, groups, n_chunks, SSM_CHUNK, ch)
                     .transpose(0, 2, 3, 1, 4).reshape(n_prompt, groups * ch))

        tm_s, sm_s, cm_s, dsk_s, coef_s = _ssm_tables(*ssm_p, dec_seq, 1)
        us_r = (u_all[n_prompt:n_tok].reshape(dec_batch, dec_seq, groups, ch)
                .transpose(2, 0, 1, 3).reshape(groups, dec_batch, dec_seq * ch))
        h0r = state_ssm_re[l].astype(F32).transpose(1, 0, 2)
        h0i = state_ssm_im[l].astype(F32).transpose(1, 0, 2)
        ys_r, h_s = _ssm_sample(us_r, jnp.concatenate([h0r, h0i], -1),
                                jnp.concatenate([h0i, h0r], -1),
                                tm_s, sm_s[..., :2 * p_dim], cm_s, dsk_s, coef_s)
        yg_sample = (ys_r.reshape(groups, dec_batch, dec_seq, ch)
                     .transpose(1, 2, 0, 3).reshape(n_sample, groups * ch))
        yg_all = jnp.concatenate(
            [yg_prompt, yg_sample, jnp.zeros((tp - n_tok, groups * ch), F32)], axis=0)

        x1 = _out_proj(x_all, a_all, yg_all, ssm_w_glu[l].astype(BF16),
                       ssm_b_glu[l].astype(F32)[None, :], w_out[l].astype(BF16),
                       ln1_g[l].astype(F32)[None, :], ln1_b[l].astype(F32)[None, :], alpha)

        keys_bf = peer_sub_keys[l].reshape(2 * peer_heads, n_keys, d_half).astype(BF16)
        n1, e1, r2, e2 = _peer_select(x1, peer_w_q[l].astype(BF16),
                                      peer_q_g[l].astype(F32)[None, :], keys_bf,
                                      peer_heads, d_key, n_keys)
        x1t = x1.T
        out_t = _peer_dense(x1t, x1t.astype(BF16), peer_u[l].astype(BF16),
                            peer_v[l].T.astype(BF16), n1, e1, r2, e2,
                            ln2_g[l].astype(F32)[:, None], ln2_b[l].astype(F32)[:, None], alpha)
        x_all = out_t.T

        outs["kp"].append(k_all[:n_prompt].reshape(batch, seq, n_heads, 2, dqk))
        outs["vp"].append(v_all[:n_prompt].reshape(batch, seq, n_heads, dv))
        outs["hpr"].append(h_p[:, :, 0, :p_dim])
        outs["hpi"].append(h_p[:, :, 0, p_dim:])
        outs["ks"].append(k_all[n_prompt:n_tok].reshape(dec_batch, dec_seq, n_heads, 2, dqk))
        outs["vs"].append(v_all[n_prompt:n_tok].reshape(dec_batch, dec_seq, n_heads, dv))
        outs["hsr"].append(h_s[:, :, :p_dim].transpose(1, 0, 2))
        outs["hsi"].append(h_s[:, :, p_dim:].transpose(1, 0, 2))

    y_prompt = x_all[:n_prompt].reshape(batch, seq, d_model)
    y_sample = x_all[n_prompt:n_tok].reshape(dec_batch, dec_seq, d_model)
    st = lambda name: jnp.stack(outs[name])
    return (y_prompt, y_sample, st("kp"), st("vp"), st("hpr"), st("hpi"),
            st("ks"), st("vs"), st("hsr"), st("hsi"))
```

```python
import functools
import math

import jax
import jax.numpy as jnp
from jax import lax
from jax.experimental import pallas as pl
from jax.experimental.pallas import tpu as pltpu

F32 = jnp.float32
BF16 = jnp.bfloat16

EPS = 1e-5
MASK_VALUE = -1e30
PEER_TOPK = 16
NEG_INF = float("-inf")

TOKEN_PAD = 512
PROJ_TM = 512
ATTN_TQ = 512
ATTN_RS = 128
SSM_CHUNK = 16
SEL_TB = 256
PEER_TB = 512
PEER_EC = 1024
BF16_SUBLANES = 16
PEER_LANE_TILE = 128
VMEM_LIMIT = 56 * 1024 * 1024

_NT = (((1,), (1,)), ((), ()))
_HI = lax.Precision.HIGHEST


def _params(*sem):
    return pltpu.CompilerParams(dimension_semantics=sem, vmem_limit_bytes=VMEM_LIMIT)


def _gelu(x):
    return 0.5 * x * (1.0 + lax.erf(x * (2.0 ** -0.5)))


def _proj_kernel(x_ref, w_ref, k_ref, v_ref, u_ref, qkv_ref, *, d_attn, scale):
    z = jnp.dot(x_ref[...].astype(BF16), w_ref[...], preferred_element_type=F32)
    k = z[:, d_attn:2 * d_attn]
    v = z[:, 2 * d_attn:3 * d_attn]
    k_ref[...] = k
    v_ref[...] = v
    u_ref[...] = z[:, 3 * d_attn:]
    qkv_ref[:, :d_attn] = (z[:, :d_attn] * scale).astype(BF16)
    qkv_ref[:, d_attn:2 * d_attn] = k.astype(BF16)
    qkv_ref[:, 2 * d_attn:] = v.astype(BF16)


def _project(x_all, w_bf, d_attn, scale):
    tp, d_model = x_all.shape
    n_out = w_bf.shape[1]
    d_u = n_out - 3 * d_attn
    tm = PROJ_TM
    return pl.pallas_call(
        functools.partial(_proj_kernel, d_attn=d_attn, scale=scale),
        grid=(tp // tm,),
        in_specs=[pl.BlockSpec((tm, d_model), lambda i: (i, 0)),
                  pl.BlockSpec((d_model, n_out), lambda i: (0, 0))],
        out_specs=[pl.BlockSpec((tm, d_attn), lambda i: (i, 0)),
                   pl.BlockSpec((tm, d_attn), lambda i: (i, 0)),
                   pl.BlockSpec((tm, d_u), lambda i: (i, 0)),
                   pl.BlockSpec((tm, 3 * d_attn), lambda i: (i, 0))],
        out_shape=[jax.ShapeDtypeStruct((tp, d_attn), F32),
                   jax.ShapeDtypeStruct((tp, d_attn), F32),
                   jax.ShapeDtypeStruct((tp, d_u), F32),
                   jax.ShapeDtypeStruct((tp, 3 * d_attn), BF16)],
        compiler_params=_params("arbitrary"),
        name="proj",
    )(x_all, w_bf)


def _lam_value(lamv_ref, lam_init):
    lv = lamv_ref[...]
    e1 = jnp.exp(jnp.sum(lv[0:1] * lv[1:2], axis=1, keepdims=True))
    e2 = jnp.exp(jnp.sum(lv[2:3] * lv[3:4], axis=1, keepdims=True))
    return e1 - e2 + lam_init


def _subln(o, g, lam_init):
    ms = jnp.mean(o * o, axis=-1, keepdims=True)
    return o * lax.rsqrt(ms + EPS) * g * (1.0 - lam_init)


def _attn_prompt_kernel(qi_ref, kj_ref, lamv_ref, g_ref, slope_ref, q_ref, k_ref, v_ref, o_ref,
                        m_ref, l_ref, acc_ref, *, tq, rs, dqk, lam_init):
    t = pl.program_id(2)
    i = qi_ref[t]
    j = kj_ref[t]
    lanes = m_ref.shape[-1]

    @pl.when(j == 0)
    def _():
        m_ref[...] = jnp.full(m_ref.shape, MASK_VALUE, F32)
        l_ref[...] = jnp.zeros(l_ref.shape, F32)
        acc_ref[...] = jnp.zeros(acc_ref.shape, F32)

    n_sub = tq // rs

    def step(diagonal):
        kcol = lax.broadcasted_iota(jnp.int32, (1, tq), 1) + (j - i) * tq
        colbias = slope_ref[...][:, :1] * kcol.astype(F32)

        def n_keys(r):
            return (r + 1) * rs if diagonal else tq

        def scores(r):
            q = q_ref[r * rs:(r + 1) * rs, :]
            lane = lax.broadcasted_iota(jnp.int32, q.shape, 1)
            zero = jnp.zeros_like(q)
            q2 = jnp.concatenate([jnp.where(lane < dqk, q, zero), jnp.where(lane < dqk, zero, q)], axis=0)
            return lax.dot_general(q2, k_ref[:n_keys(r), :], _NT, preferred_element_type=F32)

        ahead = 2
        pending = {r: scores(r) for r in range(min(ahead, n_sub))}
        for r in range(n_sub):
            if r + ahead < n_sub:
                pending[r + ahead] = scores(r + ahead)
            nk = n_keys(r)
            rows = slice(2 * r * rs, 2 * (r + 1) * rs)
            s = pending.pop(r) + colbias[:, :nk]
            if diagonal:
                row = lax.broadcasted_iota(jnp.int32, (2 * rs, nk), 0) & (rs - 1)
                col = lax.broadcasted_iota(jnp.int32, (2 * rs, nk), 1)
                s = jnp.where(row + r * rs >= col, s, MASK_VALUE)
            m_prev = m_ref[rows, :]
            m_new = jnp.maximum(m_prev, jnp.max(s, axis=1, keepdims=True))
            alpha = jnp.exp(m_prev - m_new)
            p = jnp.exp(s - pltpu.repeat(m_new, nk // lanes, 1))
            l_ref[rows, :] = alpha * l_ref[rows, :] + jnp.sum(p, axis=1, keepdims=True)
            acc_ref[rows, :] = alpha * acc_ref[rows, :] + jnp.dot(
                p.astype(BF16), v_ref[:nk, :], preferred_element_type=F32)
            m_ref[rows, :] = m_new

    @pl.when(j < i)
    def _():
        step(False)

    @pl.when(j == i)
    def _():
        step(True)
        lam = _lam_value(lamv_ref, lam_init)
        for r in range(n_sub):
            r1 = slice(2 * r * rs, (2 * r + 1) * rs)
            r2 = slice((2 * r + 1) * rs, 2 * (r + 1) * rs)
            o = acc_ref[r1, :] / l_ref[r1, :] - lam * (acc_ref[r2, :] / l_ref[r2, :])
            o_ref[r * rs:(r + 1) * rs, :] = _subln(o, g_ref[...], lam_init).astype(o_ref.dtype)


def _attn_prompt(qkv_bf, lamv, subln_g, slopes, batch, seq, n_heads, dv, dqk, lam_init):
    tq = ATTN_TQ
    nq = seq // tq
    d_attn = n_heads * dv
    pairs = [(i, j) for i in range(nq) for j in range(i + 1)]
    qi = jnp.asarray([p[0] for p in pairs], jnp.int32)
    kj = jnp.asarray([p[1] for p in pairs], jnp.int32)
    grid_spec = pltpu.PrefetchScalarGridSpec(
        num_scalar_prefetch=2,
        grid=(batch, n_heads, len(pairs)),
        in_specs=[pl.BlockSpec(lamv.shape, lambda b, h, t, qi, kj: (0, 0)),
                  pl.BlockSpec((1, dv), lambda b, h, t, qi, kj: (0, 0)),
                  pl.BlockSpec((None, 1, 128), lambda b, h, t, qi, kj: (h, 0, 0)),
                  pl.BlockSpec((tq, dv), lambda b, h, t, qi, kj: (b * nq + qi[t], h)),
                  pl.BlockSpec((tq, dv), lambda b, h, t, qi, kj: (b * nq + kj[t], n_heads + h)),
                  pl.BlockSpec((tq, dv), lambda b, h, t, qi, kj: (b * nq + kj[t], 2 * n_heads + h))],
        out_specs=pl.BlockSpec((tq, dv), lambda b, h, t, qi, kj: (b * nq + qi[t], h)),
        scratch_shapes=[pltpu.VMEM((2 * tq, dv), F32), pltpu.VMEM((2 * tq, dv), F32),
                        pltpu.VMEM((2 * tq, dv), F32)],
    )
    return pl.pallas_call(
        functools.partial(_attn_prompt_kernel, tq=tq, rs=ATTN_RS, dqk=dqk, lam_init=lam_init),
        grid_spec=grid_spec,
        out_shape=jax.ShapeDtypeStruct((batch * seq, d_attn), BF16),
        compiler_params=_params("arbitrary", "arbitrary", "arbitrary"),
        name="attn_prompt",
    )(qi, kj, lamv, subln_g, slopes, qkv_bf, qkv_bf, qkv_bf)


def _attn_sample_kernel(pt_ref, lamv_ref, g_ref, slopecol_ref, qbd_ref, kc_ref, vc_ref,
                        kn_ref, vn_ref, o_ref, m_ref, l_ref, acc_ref,
                        *, n_pages, page, past_len, n_heads, dv, n_new, lam_init):
    del pt_ref
    p_id = pl.program_id(1)
    rows_h = 2 * n_new

    @pl.when(p_id == 0)
    def _():
        m_ref[...] = jnp.full(m_ref.shape, MASK_VALUE, F32)
        l_ref[...] = jnp.zeros(l_ref.shape, F32)
        acc_ref[...] = jnp.zeros(acc_ref.shape, F32)

    def update(s, v):
        m_prev = m_ref[...]
        m_new = jnp.maximum(m_prev, jnp.max(s, axis=1, keepdims=True))
        alpha = jnp.exp(m_prev - m_new)
        p = jnp.exp(s - m_new)
        l_ref[...] = alpha * l_ref[...] + jnp.sum(p, axis=1, keepdims=True)
        pb = p.astype(BF16)
        for h in range(n_heads):
            r = slice(h * rows_h, (h + 1) * rows_h)
            acc_ref[r, :] = alpha[r] * acc_ref[r, :] + jnp.dot(
                pb[r], v[:, h * dv:(h + 1) * dv], preferred_element_type=F32)
        m_ref[...] = m_new

    @pl.when(p_id < n_pages)
    def _():
        k = kc_ref[...].astype(BF16)
        v = vc_ref[...].astype(BF16)
        s = lax.dot_general(qbd_ref[...], k, _NT, preferred_element_type=F32)
        kpos = lax.broadcasted_iota(jnp.int32, (1, page), 1) + (p_id * page - past_len)
        update(s + slopecol_ref[...] * kpos.astype(F32), v)

    @pl.when(p_id == n_pages)
    def _():
        k = kn_ref[...]
        v = vn_ref[...]
        nk = k.shape[0]
        s = lax.dot_general(qbd_ref[...], k, _NT, preferred_element_type=F32)
        col = lax.broadcasted_iota(jnp.int32, (s.shape[0], nk), 1)
        qi = lax.broadcasted_iota(jnp.int32, (s.shape[0], nk), 0) & (n_new - 1)
        s = s + slopecol_ref[...] * col.astype(F32)
        update(jnp.where(col <= qi, s, MASK_VALUE), v)
        lam = _lam_value(lamv_ref, lam_init)
        a = acc_ref[...] / l_ref[...]
        for h in range(n_heads):
            o = a[h * rows_h:h * rows_h + n_new] - lam * a[h * rows_h + n_new:(h + 1) * rows_h]
            o_ref[h * n_new:(h + 1) * n_new, :] = _subln(o, g_ref[...], lam_init).astype(o_ref.dtype)


def _attn_sample(page_table, lamv, subln_g, slopecol, qbd, cache_k, cache_v, k_new, v_new,
                 n_heads, dv, n_new, lam_init):
    dec_batch, n_pages = page_table.shape
    _, page, d_attn = cache_k.shape
    n_rows = qbd.shape[1]
    nk = k_new.shape[1]
    pt_flat = page_table.reshape(-1)

    def cache_map(b, p, pt):
        return (pt[b * n_pages + jnp.minimum(p, n_pages - 1)], 0, 0)

    kernel = functools.partial(
        _attn_sample_kernel, n_pages=n_pages, page=page, past_len=n_pages * page,
        n_heads=n_heads, dv=dv, n_new=n_new, lam_init=lam_init)
    grid_spec = pltpu.PrefetchScalarGridSpec(
        num_scalar_prefetch=1,
        grid=(dec_batch, n_pages + 1),
        in_specs=[pl.BlockSpec(lamv.shape, lambda b, p, pt: (0, 0)),
                  pl.BlockSpec((1, dv), lambda b, p, pt: (0, 0)),
                  pl.BlockSpec((n_rows, 1), lambda b, p, pt: (0, 0)),
                  pl.BlockSpec((None, n_rows, d_attn), lambda b, p, pt: (b, 0, 0)),
                  pl.BlockSpec((None, page, d_attn), cache_map),
                  pl.BlockSpec((None, page, d_attn), cache_map),
                  pl.BlockSpec((None, nk, d_attn), lambda b, p, pt: (b, 0, 0)),
                  pl.BlockSpec((None, nk, d_attn), lambda b, p, pt: (b, 0, 0))],
        out_specs=pl.BlockSpec((None, n_heads * n_new, dv), lambda b, p, pt: (b, 0, 0)),
        scratch_shapes=[pltpu.VMEM((n_rows, 1), F32), pltpu.VMEM((n_rows, 1), F32),
                        pltpu.VMEM((n_rows, dv), F32)],
    )
    return pl.pallas_call(
        kernel,
        grid_spec=grid_spec,
        out_shape=jax.ShapeDtypeStruct((dec_batch, n_heads * n_new, dv), BF16),
        compiler_params=_params("arbitrary", "arbitrary"),
        name="attn_sample",
    )(pt_flat, lamv, subln_g, slopecol, qbd, cache_k, cache_v, k_new, v_new)


def _ssm_prompt_kernel(u_ref, tm_ref, sm_ref, cm_ref, d_ref, coef_ref, y_ref, h_ref,
                       *, n_chunks, n_steps):
    u = u_ref[...]
    x = jnp.dot(u, sm_ref[...], precision=_HI, preferred_element_type=F32)
    half = x.shape[1] // 2
    x0, x1 = x[:, :half], x[:, half:]
    row = lax.broadcasted_iota(jnp.int32, x0.shape, 0)

    def shifted(a, d):
        return jnp.where(row >= d, pltpu.roll(a, d, 0), 0.0)

    y0, y1 = shifted(x0, 1), shifted(x1, 1)
    coef = coef_ref[...]
    for s in range(n_steps):
        d = 1 << s
        p = coef[2 * s:2 * s + 1]
        q = coef[2 * s + 1:2 * s + 2]
        s0, s1 = shifted(y0, d), shifted(y1, d)
        y0, y1 = y0 + p * s0 + q * s1, y1 + p * s1 - q * s0
    last = n_chunks - 1
    h_ref[...] = (coef[0:1] * y0[last:last + 1] + coef[1:2] * y1[last:last + 1]
                  + x0[last:last + 1])
    y = (jnp.dot(u, tm_ref[...], precision=_HI, preferred_element_type=F32)
         + jnp.dot(y0, cm_ref[...], precision=_HI, preferred_element_type=F32)
         + u * d_ref[...])
    y_ref[...] = _gelu(y)


def _ssm_prompt(u_r, tm, sm, cm, dsk, coef):
    batch, groups, n_chunks, width = u_r.shape
    n_steps = coef.shape[1] // 2
    state = cm.shape[1]
    return pl.pallas_call(
        functools.partial(_ssm_prompt_kernel, n_chunks=n_chunks, n_steps=n_steps),
        grid=(batch, groups),
        in_specs=[pl.BlockSpec((None, None, n_chunks, width), lambda b, g: (b, g, 0, 0)),
                  pl.BlockSpec((None, width, width), lambda b, g: (g, 0, 0)),
                  pl.BlockSpec((None, width, 2 * state), lambda b, g: (g, 0, 0)),
                  pl.BlockSpec((None, state, width), lambda b, g: (g, 0, 0)),
                  pl.BlockSpec((None, 1, width), lambda b, g: (g, 0, 0)),
                  pl.BlockSpec((None, 2 * n_steps, state), lambda b, g: (g, 0, 0))],
        out_specs=[pl.BlockSpec((None, None, n_chunks, width), lambda b, g: (b, g, 0, 0)),
                   pl.BlockSpec((None, None, 1, state), lambda b, g: (b, g, 0, 0))],
        out_shape=[jax.ShapeDtypeStruct((batch, groups, n_chunks, width), F32),
                   jax.ShapeDtypeStruct((batch, groups, 1, state), F32)],
        compiler_params=_params("arbitrary", "arbitrary"),
        name="ssm_prompt",
    )(u_r, tm, sm, cm, dsk, coef)


def _ssm_sample_kernel(u_ref, h0_ref, h0s_ref, tm_ref, sm_ref, cm_ref, d_ref, coef_ref,
                       y_ref, h_ref):
    u = u_ref[...]
    h0 = h0_ref[...]
    coef = coef_ref[...]
    x = jnp.dot(u, sm_ref[...], precision=_HI, preferred_element_type=F32)
    h_ref[...] = coef[0:1] * h0 + coef[1:2] * h0s_ref[...] + x
    y = (jnp.dot(u, tm_ref[...], precision=_HI, preferred_element_type=F32)
         + jnp.dot(h0, cm_ref[...], precision=_HI, preferred_element_type=F32)
         + u * d_ref[...])
    y_ref[...] = _gelu(y)


def _ssm_sample(u_r, h0, h0s, tm, sm, cm, dsk, coef):
    groups, seqs, width = u_r.shape
    state = h0.shape[2]
    return pl.pallas_call(
        _ssm_sample_kernel,
        grid=(groups,),
        in_specs=[pl.BlockSpec((None, seqs, width), lambda g: (g, 0, 0)),
                  pl.BlockSpec((None, seqs, state), lambda g: (g, 0, 0)),
                  pl.BlockSpec((None, seqs, state), lambda g: (g, 0, 0)),
                  pl.BlockSpec((None, width, width), lambda g: (g, 0, 0)),
                  pl.BlockSpec((None, width, state), lambda g: (g, 0, 0)),
                  pl.BlockSpec((None, state, width), lambda g: (g, 0, 0)),
                  pl.BlockSpec((None, 1, width), lambda g: (g, 0, 0)),
                  pl.BlockSpec((None, 2, state), lambda g: (g, 0, 0))],
        out_specs=[pl.BlockSpec((None, seqs, width), lambda g: (g, 0, 0)),
                   pl.BlockSpec((None, seqs, state), lambda g: (g, 0, 0))],
        out_shape=[jax.ShapeDtypeStruct((groups, seqs, width), F32),
                   jax.ShapeDtypeStruct((groups, seqs, state), F32)],
        compiler_params=_params("arbitrary"),
        name="ssm_sample",
    )(u_r, h0, h0s, tm, sm, cm, dsk, coef)


def _ssm_tables(lam_re, lam_im, log_dt, b_re, b_im, c_re, c_im, d_skip, chunk, n_steps):
    groups, p_dim, ch = b_re.shape
    lr, li = lam_re.astype(F32), lam_im.astype(F32)
    dt = jnp.exp(log_dt.astype(F32))[:, None]
    mag = jnp.exp(lr * dt)
    a_re, a_im = mag * jnp.cos(li * dt), mag * jnp.sin(li * dt)
    den = lr * lr + li * li
    n_re, n_im = a_re - 1.0, a_im
    g_re = (n_re * lr + n_im * li) / den
    g_im = (n_im * lr - n_re * li) / den
    br, bi = b_re.astype(F32), b_im.astype(F32)
    bb_re = g_re[..., None] * br - g_im[..., None] * bi
    bb_im = g_re[..., None] * bi + g_im[..., None] * br

    def power(n):
        n = n.astype(F32)[:, None, None]
        m = jnp.exp(lr * dt * n)
        return m * jnp.cos(li * dt * n), m * jnp.sin(li * dt * n)

    taus = jnp.arange(chunk + 1)
    pw_re, pw_im = power(taus)
    ab_re = pw_re[:chunk, :, :, None] * bb_re - pw_im[:chunk, :, :, None] * bb_im
    ab_im = pw_re[:chunk, :, :, None] * bb_im + pw_im[:chunk, :, :, None] * bb_re
    cr, ci = c_re.astype(F32), c_im.astype(F32)
    kern = (jnp.einsum('gcp,tgpd->gtcd', cr, ab_re, precision=_HI)
            - jnp.einsum('gcp,tgpd->gtcd', ci, ab_im, precision=_HI))
    s_idx = jnp.arange(chunk)[:, None]
    t_idx = jnp.arange(chunk)[None, :]
    lag = t_idx - s_idx
    tm = jnp.where((lag >= 0)[None, :, :, None, None], kern[:, jnp.maximum(lag, 0)], 0.0)
    tm = tm.transpose(0, 1, 4, 2, 3).reshape(groups, chunk * ch, chunk * ch)
    sm_re = ab_re[::-1].transpose(1, 0, 3, 2).reshape(groups, chunk * ch, p_dim)
    sm_im = ab_im[::-1].transpose(1, 0, 3, 2).reshape(groups, chunk * ch, p_dim)
    sm = jnp.concatenate([sm_re, sm_im, sm_im, sm_re], axis=-1)
    ca_re = cr[:, None] * pw_re[1:].transpose(1, 0, 2)[:, :, None, :] \
        - ci[:, None] * pw_im[1:].transpose(1, 0, 2)[:, :, None, :]
    ca_im = cr[:, None] * pw_im[1:].transpose(1, 0, 2)[:, :, None, :] \
        + ci[:, None] * pw_re[1:].transpose(1, 0, 2)[:, :, None, :]
    cm = jnp.concatenate([ca_re, -ca_im], axis=-1)
    cm = cm.transpose(0, 3, 1, 2).reshape(groups, 2 * p_dim, chunk * ch)
    dsk = jnp.tile(d_skip.astype(F32), (1, chunk)).reshape(groups, 1, chunk * ch)
    steps = chunk * (2 ** jnp.arange(n_steps))
    sp_re, sp_im = power(steps)
    p_rows = jnp.concatenate([sp_re, sp_re], axis=-1)
    q_rows = jnp.concatenate([-sp_im, sp_im], axis=-1)
    coef = jnp.stack([p_rows, q_rows], axis=1).reshape(2 * n_steps, groups, 2 * p_dim)
    return tm, sm, cm, dsk, coef.transpose(1, 0, 2)


def _layer_norm(y, g, b, axis):
    mu = jnp.mean(y, axis=axis, keepdims=True)
    var = jnp.mean(jnp.square(y - mu), axis=axis, keepdims=True)
    return (y - mu) * lax.rsqrt(var + EPS) * g + b


def _out_kernel(x_ref, a_ref, yg_ref, wglu_ref, bglu_ref, wo_ref, g_ref, b_ref, x1_ref,
                *, d_attn, alpha):
    yg = yg_ref[...]
    gate = jax.nn.sigmoid(
        jnp.dot(yg.astype(BF16), wglu_ref[...], preferred_element_type=F32) + bglu_ref[...])
    s = (yg * gate).astype(BF16)
    h = (jnp.dot(a_ref[...], wo_ref[:d_attn, :], preferred_element_type=F32)
         + jnp.dot(s, wo_ref[d_attn:, :], preferred_element_type=F32))
    x1_ref[...] = _layer_norm(alpha * x_ref[...] + h, g_ref[...], b_ref[...], -1)


def _out_proj(x_all, a_all, yg_all, wglu_bf, bglu, wo_bf, ln_g, ln_b, alpha):
    tp, d_model = x_all.shape
    d_attn = a_all.shape[1]
    d_ssm = yg_all.shape[1]
    tm = PROJ_TM
    row = lambda i: (i, 0)
    fixed = lambda i: (0, 0)
    return pl.pallas_call(
        functools.partial(_out_kernel, d_attn=d_attn, alpha=alpha),
        grid=(tp // tm,),
        in_specs=[pl.BlockSpec((tm, d_model), row), pl.BlockSpec((tm, d_attn), row),
                  pl.BlockSpec((tm, d_ssm), row), pl.BlockSpec((d_ssm, d_ssm), fixed),
                  pl.BlockSpec((1, d_ssm), fixed), pl.BlockSpec((d_model, d_model), fixed),
                  pl.BlockSpec((1, d_model), fixed), pl.BlockSpec((1, d_model), fixed)],
        out_specs=pl.BlockSpec((tm, d_model), row),
        out_shape=jax.ShapeDtypeStruct((tp, d_model), F32),
        compiler_params=_params("arbitrary"),
        name="out_proj",
    )(x_all, a_all, yg_all, wglu_bf, bglu, wo_bf, ln_g, ln_b)


def _top_k_rows(s, k):
    n = s.shape[0]
    iota = lax.broadcasted_iota(jnp.int32, s.shape, 0).astype(F32)
    rank = jnp.full(s.shape, float(k), F32)
    vals, idxs = [], []
    for r in range(k):
        mx = jnp.max(s, axis=0, keepdims=True)
        ix = jnp.min(jnp.where(s == mx, iota, float(n)), axis=0, keepdims=True)
        hit = iota == ix
        rank = jnp.where(hit, float(r), rank)
        s = jnp.where(hit, NEG_INF, s)
        vals.append(mx)
        idxs.append(ix)
    return vals, idxs, rank


def _staircase(k):
    return [(a, b) for a in range(k) for b in range(k) if (a + 1) * (b + 1) <= k]


def _peer_sel_kernel(x1_ref, wq_ref, qg_ref, keys_ref, n1_ref, e1_ref, r2_ref, e2_ref,
                     *, n_heads, d_key, topk):
    qp = jnp.dot(x1_ref[...].astype(BF16), wq_ref[...], preferred_element_type=F32)
    d_half = d_key // 2
    pairs = _staircase(topk)
    n_pairs = len(pairs)
    n_rows = -(-n_pairs // 8) * 8
    tb = qp.shape[0]
    for h in range(n_heads):
        qh = qp[:, h * d_key:(h + 1) * d_key]
        qn = qh * lax.rsqrt(jnp.mean(qh * qh, axis=-1, keepdims=True) + EPS) * qg_ref[...]
        qb = qn.astype(BF16)
        s1 = lax.dot_general(keys_ref[2 * h], qb[:, :d_half], _NT, preferred_element_type=F32)
        s2 = lax.dot_general(keys_ref[2 * h + 1], qb[:, d_half:], _NT, preferred_element_type=F32)
        t1, i1, _ = _top_k_rows(s1, topk)
        t2, _, r2 = _top_k_rows(s2, topk)
        cand = jnp.concatenate(
            [t1[a] + t2[b] for a, b in pairs]
            + [jnp.full((n_rows - n_pairs, tb), NEG_INF, F32)], axis=0)
        iota = lax.broadcasted_iota(jnp.int32, cand.shape, 0).astype(F32)
        work = cand
        z = jnp.zeros((1, tb), F32)
        m = t1[0] + t2[0]
        for _ in range(topk):
            mx = jnp.max(work, axis=0, keepdims=True)
            ix = jnp.min(jnp.where(work == mx, iota, float(n_rows)), axis=0, keepdims=True)
            work = jnp.where(iota == ix, NEG_INF, work)
            z = z + jnp.exp(mx - m)
        chosen = jnp.where((work == NEG_INF) & (iota < float(n_pairs)), 1.0, 0.0)
        key_iota = lax.broadcasted_iota(jnp.int32, s1.shape, 0).astype(F32)
        n1 = jnp.zeros(s1.shape, F32)
        off = 0
        for a in range(topk):
            cnt = sum(1 for pa, _ in pairs if pa == a)
            n_a = jnp.sum(chosen[off:off + cnt], axis=0, keepdims=True)
            off += cnt
            n1 = jnp.where(key_iota == i1[a], n_a, n1)
        n1_ref[h] = n1.astype(n1_ref.dtype)
        e1_ref[h] = jnp.exp(s1 - t1[0]).astype(e1_ref.dtype)
        r2_ref[h] = pltpu.bitcast(r2.astype(BF16), r2_ref.dtype)
        e2_ref[h] = pltpu.bitcast((jnp.exp(s2 - t2[0]) / z).astype(BF16), e2_ref.dtype)


def _peer_select(x1, wq_bf, q_g, keys_bf, n_heads, d_key, n_keys):
    tp, d_model = x1.shape
    tb = SEL_TB
    row_spec = pl.BlockSpec((n_heads, n_keys, tb), lambda i: (0, 0, i))
    tile_spec = pl.BlockSpec((n_heads, n_keys // 2, tb), lambda i: (0, 0, i))
    out_shape = [jax.ShapeDtypeStruct((n_heads, n_keys, tp), F32)] * 2 + [
        jax.ShapeDtypeStruct((n_heads, n_keys // 2, tp), jnp.uint32)] * 2
    return pl.pallas_call(
        functools.partial(_peer_sel_kernel, n_heads=n_heads, d_key=d_key, topk=PEER_TOPK),
        grid=(tp // tb,),
        in_specs=[pl.BlockSpec((tb, d_model), lambda i: (i, 0)),
                  pl.BlockSpec(wq_bf.shape, lambda i: (0, 0)),
                  pl.BlockSpec((1, d_key), lambda i: (0, 0)),
                  pl.BlockSpec(keys_bf.shape, lambda i: (0, 0, 0))],
        out_specs=[row_spec, row_spec, tile_spec, tile_spec],
        out_shape=out_shape,
        compiler_params=_params("arbitrary"),
        name="peer_select",
    )(x1, wq_bf, q_g, keys_bf)


def _peer_dense_kernel(xt_ref, xtb_ref, u_ref, vt_ref, n1_ref, e1_ref, r2_ref, e2_ref,
                       g_ref, b_ref, o_ref, acc_ref, act_ref, wg_ref,
                       *, n_heads, n_keys, lane_tile, alpha):
    c = pl.program_id(1)
    n_chunks = pl.num_programs(1) - 1
    n_i = act_ref.shape[0] // n_keys
    tb = act_ref.shape[1]
    zero = jnp.zeros((n_keys, lane_tile), BF16)

    @pl.when(c == 0)
    def _():
        acc_ref[...] = jnp.zeros(acc_ref.shape, F32)
        wg_ref[1] = jnp.zeros(wg_ref.shape[1:], BF16)

    def row_tile(ref, h, il, cols):
        row = jnp.broadcast_to(ref[h, il:il + 1, cols], (BF16_SUBLANES, lane_tile)).astype(BF16)
        return pltpu.repeat(row, n_keys // BF16_SUBLANES, 0)

    def pv_previous():
        acc_ref[...] += jnp.dot(vt_ref[...], wg_ref[(c + 1) % 2], preferred_element_type=F32)

    @pl.when(c < n_chunks)
    def _():
        pv_previous()
        act_ref[...] = jnp.dot(u_ref[...], xtb_ref[...], preferred_element_type=F32)
        cur = c % 2
        for tc in range(tb // lane_tile):
            cols = slice(tc * lane_tile, (tc + 1) * lane_tile)
            for il in range(n_i):
                w = zero
                for h in range(n_heads):
                    sel = pltpu.bitcast(r2_ref[h, :, cols], BF16) < row_tile(n1_ref, h, il, cols)
                    e2 = pltpu.bitcast(e2_ref[h, :, cols], BF16)
                    w = w + jnp.where(sel, e2, zero) * row_tile(e1_ref, h, il, cols)
                rows = slice(il * n_keys, (il + 1) * n_keys)
                wg_ref[cur, rows, cols] = w * _gelu(act_ref[rows, cols]).astype(BF16)

    @pl.when(c == n_chunks)
    def _():
        pv_previous()
        o_ref[...] = _layer_norm(alpha * xt_ref[...] + acc_ref[...], g_ref[...], b_ref[...], 0)


def _peer_dense(x1t, x1t_bf, u_bf, vt_bf, n1, e1, r2, e2, ln_g, ln_b, alpha):
    d_model, tp = x1t.shape
    n_heads, n_keys, _ = n1.shape
    n_exp = u_bf.shape[0]
    tb, ec = PEER_TB, PEER_EC
    n_i = ec // n_keys
    n_chunks = n_exp // ec
    tok = lambda t, c: (0, t)
    cur = lambda c: jnp.minimum(c, n_chunks - 1)
    prev = lambda c: jnp.maximum(c - 1, 0)
    return pl.pallas_call(
        functools.partial(_peer_dense_kernel, n_heads=n_heads, n_keys=n_keys,
                          lane_tile=PEER_LANE_TILE, alpha=alpha),
        grid=(tp // tb, n_chunks + 1),
        in_specs=[pl.BlockSpec((d_model, tb), tok),
                  pl.BlockSpec((d_model, tb), tok),
                  pl.BlockSpec((ec, d_model), lambda t, c: (cur(c), 0)),
                  pl.BlockSpec((d_model, ec), lambda t, c: (0, prev(c))),
                  pl.BlockSpec((n_heads, n_i, tb), lambda t, c: (0, cur(c), t)),
                  pl.BlockSpec((n_heads, n_i, tb), lambda t, c: (0, cur(c), t)),
                  pl.BlockSpec((n_heads, n_keys // 2, tb), lambda t, c: (0, 0, t)),
                  pl.BlockSpec((n_heads, n_keys // 2, tb), lambda t, c: (0, 0, t)),
                  pl.BlockSpec((d_model, 1), lambda t, c: (0, 0)),
                  pl.BlockSpec((d_model, 1), lambda t, c: (0, 0))],
        out_specs=pl.BlockSpec((d_model, tb), tok),
        out_shape=jax.ShapeDtypeStruct((d_model, tp), F32),
        scratch_shapes=[pltpu.VMEM((d_model, tb), F32), pltpu.VMEM((ec, tb), F32),
                        pltpu.VMEM((2, ec, tb), BF16)],
        compiler_params=_params("arbitrary", "arbitrary"),
        name="peer_dense",
    )(x1t, x1t_bf, u_bf, vt_bf, n1, e1, r2, e2, ln_g, ln_b)


def kernel(x_prompt, x_sample, cache_k, cache_v, state_ssm_re, state_ssm_im, page_table,
           w_in, lambda_q1, lambda_k1, lambda_q2, lambda_k2, attn_subln_g,
           ssm_lambda_re, ssm_lambda_im, ssm_log_dt, ssm_b_re, ssm_b_im, ssm_c_re, ssm_c_im,
           ssm_d, ssm_w_glu, ssm_b_glu, w_out, ln1_g, ln1_b,
           peer_w_q, peer_q_g, peer_sub_keys, peer_u, peer_v, ln2_g, ln2_b):
    depth = w_in.shape[0]
    batch, seq, d_model = x_prompt.shape
    dec_batch, dec_seq, _ = x_sample.shape
    _, n_phys, page, n_heads, _, dqk = cache_k.shape
    dv = cache_v.shape[-1]
    d_attn = n_heads * dv
    groups, p_dim = state_ssm_re.shape[2:]
    ch = ssm_b_re.shape[-1]
    peer_heads, _, n_keys, d_half = peer_sub_keys.shape[1:]
    d_key = 2 * d_half
    alpha = (2.0 * depth) ** 0.25
    scale = dqk ** -0.5

    n_prompt = batch * seq
    n_sample = dec_batch * dec_seq
    n_tok = n_prompt + n_sample
    tp = -(-n_tok // TOKEN_PAD) * TOKEN_PAD
    n_chunks = seq // SSM_CHUNK
    n_steps = int(math.log2(n_chunks))
    assert n_chunks == 1 << n_steps and seq % ATTN_TQ == 0 and n_prompt % TOKEN_PAD == 0

    x_all = jnp.concatenate(
        [x_prompt.reshape(n_prompt, d_model), x_sample.reshape(n_sample, d_model),
         jnp.zeros((tp - n_tok, d_model), x_prompt.dtype)], axis=0)
    slope_vals = 2.0 ** (-8.0 * jnp.arange(1, n_heads + 1, dtype=F32) / n_heads)
    slopes = jnp.broadcast_to(slope_vals[:, None, None], (n_heads, 1, 128))
    slopecol = jnp.repeat(slope_vals, 2 * dec_seq)[:, None]
    eye_hc = jnp.eye(2 * n_heads, dtype=F32)

    outs = {name: [] for name in ("kp", "vp", "hpr", "hpi", "ks", "vs", "hsr", "hsi")}
    for l in range(depth):
        lam_init = 0.8 - 0.6 * math.exp(-0.3 * l)
        lamv = jnp.stack([lambda_q1[l], lambda_k1[l], lambda_q2[l], lambda_k2[l]]).astype(F32)
        subln_g = attn_subln_g[l].astype(F32)[None, :]

        k_all, v_all, u_all, qkv_bf = _project(x_all, w_in[l].astype(BF16), d_attn, scale)

        a_prompt = _attn_prompt(qkv_bf, lamv, subln_g, slopes, batch, seq, n_heads, dv, dqk,
                                lam_init)

        qkv_s = qkv_bf[n_prompt:n_tok].reshape(dec_batch, dec_seq, 3 * d_attn)
        q_s = qkv_s[..., :d_attn].reshape(dec_batch, dec_seq, 2 * n_heads, dqk)
        qbd = (q_s.transpose(0, 2, 1, 3)[:, :, :, None, :].astype(F32)
               * eye_hc[None, :, None, :, None]).astype(BF16)
        qbd = qbd.reshape(dec_batch, 2 * n_heads * dec_seq, d_attn)
        pad_new = ((0, 0), (0, 128 - dec_seq), (0, 0))
        k_new = jnp.pad(qkv_s[..., d_attn:2 * d_attn], pad_new)
        v_new = jnp.pad(qkv_s[..., 2 * d_attn:], pad_new)
        a_s = _attn_sample(page_table, lamv, subln_g, slopecol, qbd,
                           cache_k[l].reshape(n_phys, page, d_attn),
                           cache_v[l].reshape(n_phys, page, d_attn),
                           k_new, v_new, n_heads, dv, dec_seq, lam_init)
        a_sample = a_s.reshape(dec_batch, n_heads, dec_seq, dv).transpose(0, 2, 1, 3)
        a_all = jnp.concatenate(
            [a_prompt, a_sample.reshape(n_sample, d_attn),
             jnp.zeros((tp - n_tok, d_attn), BF16)], axis=0)

        ssm_p = (ssm_lambda_re[l], ssm_lambda_im[l], ssm_log_dt[l], ssm_b_re[l], ssm_b_im[l],
                 ssm_c_re[l], ssm_c_im[l], ssm_d[l])
        tm, sm, cm, dsk, coef = _ssm_tables(*ssm_p, SSM_CHUNK, n_steps)
        u_r = (u_all[:n_prompt].reshape(batch, n_chunks, SSM_CHUNK, groups, ch)
               .transpose(0, 3, 1, 2, 4).reshape(batch, groups, n_chunks, SSM_CHUNK * ch))
        y_r, h_p = _ssm_prompt(u_r, tm, sm, cm, dsk, coef)
        yg_prompt = (y_r.reshape(batch, groups, n_chunks, SSM_CHUNK, ch)
                     .transpose(0, 2, 3, 1, 4).reshape(n_prompt, groups * ch))

        tm_s, sm_s, cm_s, dsk_s, coef_s = _ssm_tables(*ssm_p, dec_seq, 1)
        us_r = (u_all[n_prompt:n_tok].reshape(dec_batch, dec_seq, groups, ch)
                .transpose(2, 0, 1, 3).reshape(groups, dec_batch, dec_seq * ch))
        h0r = state_ssm_re[l].astype(F32).transpose(1, 0, 2)
        h0i = state_ssm_im[l].astype(F32).transpose(1, 0, 2)
        ys_r, h_s = _ssm_sample(us_r, jnp.concatenate([h0r, h0i], -1),
                                jnp.concatenate([h0i, h0r], -1),
                                tm_s, sm_s[..., :2 * p_dim], cm_s, dsk_s, coef_s)
        yg_sample = (ys_r.reshape(groups, dec_batch, dec_seq, ch)
                     .transpose(1, 2, 0, 3).reshape(n_sample, groups * ch))
        yg_all = jnp.concatenate(
            [yg_prompt, yg_sample, jnp.zeros((tp - n_tok, groups * ch), F32)], axis=0)

        x1 = _out_proj(x_all, a_all, yg_all, ssm_w_glu[l].astype(BF16),
                       ssm_b_glu[l].astype(F32)[None, :], w_out[l].astype(BF16),
                       ln1_g[l].astype(F32)[None, :], ln1_b[l].astype(F32)[None, :], alpha)

        keys_bf = peer_sub_keys[l].reshape(2 * peer_heads, n_keys, d_half).astype(BF16)
        n1, e1, r2, e2 = _peer_select(x1, peer_w_q[l].astype(BF16),
                                      peer_q_g[l].astype(F32)[None, :], keys_bf,
                                      peer_heads, d_key, n_keys)
        x1t = x1.T
        out_t = _peer_dense(x1t, x1t.astype(BF16), peer_u[l].astype(BF16),
                            peer_v[l].T.astype(BF16), n1, e1, r2, e2,
                            ln2_g[l].astype(F32)[:, None], ln2_b[l].astype(F32)[:, None], alpha)
        x_all = out_t.T

        outs["kp"].append(k_all[:n_prompt].reshape(batch, seq, n_heads, 2, dqk))
        outs["vp"].append(v_all[:n_prompt].reshape(batch, seq, n_heads, dv))
        outs["hpr"].append(h_p[:, :, 0, :p_dim])
        outs["hpi"].append(h_p[:, :, 0, p_dim:])
        outs["ks"].append(k_all[n_prompt:n_tok].reshape(dec_batch, dec_seq, n_heads, 2, dqk))
        outs["vs"].append(v_all[n_prompt:n_tok].reshape(dec_batch, dec_seq, n_heads, dv))
        outs["hsr"].append(h_s[:, :, :p_dim].transpose(1, 0, 2))
        outs["hsi"].append(h_s[:, :, p_dim:].transpose(1, 0, 2))

    y_prompt = x_all[:n_prompt].reshape(batch, seq, d_model)
    y_sample = x_all[n_prompt:n_tok].reshape(dec_batch, dec_seq, d_model)
    st = lambda name: jnp.stack(outs[name])
    return (y_prompt, y_sample, st("kp"), st("vp"), st("hpr"), st("hpi"),
            st("ks"), st("vs"), st("hsr"), st("hsi"))
```

```python
import functools
import math

import jax
import jax.numpy as jnp
from jax import lax
from jax.experimental import pallas as pl
from jax.experimental.pallas import tpu as pltpu

F32 = jnp.float32
BF16 = jnp.bfloat16

EPS = 1e-5
MASK_VALUE = -1e30
PEER_TOPK = 16
NEG_INF = float("-inf")

TOKEN_PAD = 512
PROJ_TM = 512
ATTN_TQ = 512
ATTN_RS = 128
SAMPLE_PAGES = 8
SSM_CHUNK = 16
SEL_TB = 256
PEER_TB = 512
PEER_EC = 1024
BF16_SUBLANES = 16
PEER_LANE_TILE = 128
VMEM_LIMIT = 56 * 1024 * 1024

_NT = (((1,), (1,)), ((), ()))
_HI = lax.Precision.HIGHEST


def _params(*sem):
    return pltpu.CompilerParams(dimension_semantics=sem, vmem_limit_bytes=VMEM_LIMIT)


def _gelu(x):
    return 0.5 * x * (1.0 + lax.erf(x * (2.0 ** -0.5)))


def _proj_kernel(x_ref, w_ref, k_ref, v_ref, u_ref, qkv_ref, *, d_attn, scale):
    z = jnp.dot(x_ref[...].astype(BF16), w_ref[...], preferred_element_type=F32)
    k = z[:, d_attn:2 * d_attn]
    v = z[:, 2 * d_attn:3 * d_attn]
    k_ref[...] = k
    v_ref[...] = v
    u_ref[...] = z[:, 3 * d_attn:]
    qkv_ref[:, :d_attn] = (z[:, :d_attn] * scale).astype(BF16)
    qkv_ref[:, d_attn:2 * d_attn] = k.astype(BF16)
    qkv_ref[:, 2 * d_attn:] = v.astype(BF16)


def _project(x_all, w_bf, d_attn, scale):
    tp, d_model = x_all.shape
    n_out = w_bf.shape[1]
    d_u = n_out - 3 * d_attn
    tm = PROJ_TM
    return pl.pallas_call(
        functools.partial(_proj_kernel, d_attn=d_attn, scale=scale),
        grid=(tp // tm,),
        in_specs=[pl.BlockSpec((tm, d_model), lambda i: (i, 0)),
                  pl.BlockSpec((d_model, n_out), lambda i: (0, 0))],
        out_specs=[pl.BlockSpec((tm, d_attn), lambda i: (i, 0)),
                   pl.BlockSpec((tm, d_attn), lambda i: (i, 0)),
                   pl.BlockSpec((tm, d_u), lambda i: (i, 0)),
                   pl.BlockSpec((tm, 3 * d_attn), lambda i: (i, 0))],
        out_shape=[jax.ShapeDtypeStruct((tp, d_attn), F32),
                   jax.ShapeDtypeStruct((tp, d_attn), F32),
                   jax.ShapeDtypeStruct((tp, d_u), F32),
                   jax.ShapeDtypeStruct((tp, 3 * d_attn), BF16)],
        compiler_params=_params("arbitrary"),
        name="proj",
    )(x_all, w_bf)


def _lam_value(lamv_ref, lam_init):
    lv = lamv_ref[...]
    e1 = jnp.exp(jnp.sum(lv[0:1] * lv[1:2], axis=1, keepdims=True))
    e2 = jnp.exp(jnp.sum(lv[2:3] * lv[3:4], axis=1, keepdims=True))
    return e1 - e2 + lam_init


def _subln(o, g, lam_init):
    ms = jnp.mean(o * o, axis=-1, keepdims=True)
    return o * lax.rsqrt(ms + EPS) * g * (1.0 - lam_init)


def _attn_prompt_kernel(qi_ref, kj_ref, lamv_ref, g_ref, slope_ref, q_ref, k_ref, v_ref, o_ref,
                        m_ref, l_ref, acc_ref, *, tq, rs, dqk, lam_init):
    t = pl.program_id(2)
    i = qi_ref[t]
    j = kj_ref[t]
    lanes = m_ref.shape[-1]

    @pl.when(j == 0)
    def _():
        m_ref[...] = jnp.full(m_ref.shape, MASK_VALUE, F32)
        l_ref[...] = jnp.zeros(l_ref.shape, F32)
        acc_ref[...] = jnp.zeros(acc_ref.shape, F32)

    n_sub = tq // rs

    def step(diagonal):
        kcol = lax.broadcasted_iota(jnp.int32, (1, tq), 1) + (j - i) * tq
        colbias = slope_ref[...][:, :1] * kcol.astype(F32)

        def n_keys(r):
            return (r + 1) * rs if diagonal else tq

        def scores(r):
            q = q_ref[r * rs:(r + 1) * rs, :]
            lane = lax.broadcasted_iota(jnp.int32, q.shape, 1)
            zero = jnp.zeros_like(q)
            q2 = jnp.concatenate([jnp.where(lane < dqk, q, zero), jnp.where(lane < dqk, zero, q)], axis=0)
            return lax.dot_general(q2, k_ref[:n_keys(r), :], _NT, preferred_element_type=F32)

        ahead = 2
        pending = {r: scores(r) for r in range(min(ahead, n_sub))}
        for r in range(n_sub):
            if r + ahead < n_sub:
                pending[r + ahead] = scores(r + ahead)
            nk = n_keys(r)
            rows = slice(2 * r * rs, 2 * (r + 1) * rs)
            s = pending.pop(r) + colbias[:, :nk]
            if diagonal:
                row = lax.broadcasted_iota(jnp.int32, (2 * rs, nk), 0) & (rs - 1)
                col = lax.broadcasted_iota(jnp.int32, (2 * rs, nk), 1)
                s = jnp.where(row + r * rs >= col, s, MASK_VALUE)
            m_prev = m_ref[rows, :]
            m_new = jnp.maximum(m_prev, jnp.max(s, axis=1, keepdims=True))
            alpha = jnp.exp(m_prev - m_new)
            p = jnp.exp(s - jnp.tile(m_new, (1, nk // lanes)))
            l_ref[rows, :] = alpha * l_ref[rows, :] + jnp.sum(p, axis=1, keepdims=True)
            acc_ref[rows, :] = alpha * acc_ref[rows, :] + jnp.dot(
                p.astype(BF16), v_ref[:nk, :], preferred_element_type=F32)
            m_ref[rows, :] = m_new

    @pl.when(j < i)
    def _():
        step(False)

    @pl.when(j == i)
    def _():
        step(True)
        lam = _lam_value(lamv_ref, lam_init)
        for r in range(n_sub):
            r1 = slice(2 * r * rs, (2 * r + 1) * rs)
            r2 = slice((2 * r + 1) * rs, 2 * (r + 1) * rs)
            o = acc_ref[r1, :] / l_ref[r1, :] - lam * (acc_ref[r2, :] / l_ref[r2, :])
            o_ref[r * rs:(r + 1) * rs, :] = _subln(o, g_ref[...], lam_init).astype(o_ref.dtype)


def _attn_prompt(qkv_bf, lamv, subln_g, slopes, batch, seq, n_heads, dv, dqk, lam_init):
    tq = ATTN_TQ
    nq = seq // tq
    d_attn = n_heads * dv
    pairs = [(i, j) for i in range(nq) for j in range(i + 1)]
    qi = jnp.asarray([p[0] for p in pairs], jnp.int32)
    kj = jnp.asarray([p[1] for p in pairs], jnp.int32)
    grid_spec = pltpu.PrefetchScalarGridSpec(
        num_scalar_prefetch=2,
        grid=(batch, n_heads, len(pairs)),
        in_specs=[pl.BlockSpec(lamv.shape, lambda b, h, t, qi, kj: (0, 0)),
                  pl.BlockSpec((1, dv), lambda b, h, t, qi, kj: (0, 0)),
                  pl.BlockSpec((None, 1, 128), lambda b, h, t, qi, kj: (h, 0, 0)),
                  pl.BlockSpec((tq, dv), lambda b, h, t, qi, kj: (b * nq + qi[t], h)),
                  pl.BlockSpec((tq, dv), lambda b, h, t, qi, kj: (b * nq + kj[t], n_heads + h)),
                  pl.BlockSpec((tq, dv), lambda b, h, t, qi, kj: (b * nq + kj[t], 2 * n_heads + h))],
        out_specs=pl.BlockSpec((tq, dv), lambda b, h, t, qi, kj: (b * nq + qi[t], h)),
        scratch_shapes=[pltpu.VMEM((2 * tq, dv), F32), pltpu.VMEM((2 * tq, dv), F32),
                        pltpu.VMEM((2 * tq, dv), F32)],
    )
    return pl.pallas_call(
        functools.partial(_attn_prompt_kernel, tq=tq, rs=ATTN_RS, dqk=dqk, lam_init=lam_init),
        grid_spec=grid_spec,
        out_shape=jax.ShapeDtypeStruct((batch * seq, d_attn), BF16),
        compiler_params=_params("arbitrary", "arbitrary", "arbitrary"),
        name="attn_prompt",
    )(qi, kj, lamv, subln_g, slopes, qkv_bf, qkv_bf, qkv_bf)


def _attn_sample_kernel(pt_ref, lamv_ref, g_ref, slopecol_ref, q_ref, kn_ref, vn_ref, *rest,
                        pp, page, past_len, n_heads, n_new, lam_init):
    del pt_ref
    k_refs, v_refs = rest[:pp], rest[pp:2 * pp]
    o_ref, m_ref, l_ref, acc_ref = rest[2 * pp:]
    step = pl.program_id(1)
    n_rows = 2 * n_heads * n_new
    lanes = m_ref.shape[-1]
    head_shift = n_heads.bit_length() - 1
    new_shift = n_new.bit_length() - 1

    @pl.when(step == 0)
    def _():
        m_ref[...] = jnp.full(m_ref.shape, MASK_VALUE, F32)
        l_ref[...] = jnp.zeros(l_ref.shape, F32)
        acc_ref[...] = jnp.zeros(acc_ref.shape, F32)

    def scores(k0, k1):
        s0 = lax.dot_general(q_ref[0], k0, _NT, preferred_element_type=F32)
        s1 = lax.dot_general(q_ref[1], k1, _NT, preferred_element_type=F32)
        return jnp.concatenate([s0, s1], axis=0)

    def geometry(n_cols):
        row = lax.broadcasted_iota(jnp.int32, (n_rows, n_cols), 0)
        col = lax.broadcasted_iota(jnp.int32, (n_rows, n_cols), 1)
        head_ok = (col & (n_heads - 1)) == ((row & (n_heads * n_new - 1)) >> new_shift)
        return head_ok, col >> head_shift, row & (n_new - 1)

    def update(s, v):
        m_prev = m_ref[...]
        m_new = jnp.maximum(m_prev, jnp.max(s, axis=1, keepdims=True))
        alpha = jnp.exp(m_prev - m_new)
        p = jnp.exp(s - jnp.tile(m_new, (1, s.shape[1] // lanes)))
        l_ref[...] = alpha * l_ref[...] + jnp.sum(p, axis=1, keepdims=True)
        acc_ref[...] = alpha * acc_ref[...] + jnp.dot(p.astype(BF16), v, preferred_element_type=F32)
        m_ref[...] = m_new

    n_cols = page * n_heads
    head_ok, _, _ = geometry(n_cols)
    key = lax.broadcasted_iota(jnp.int32, (1, n_cols), 1) >> head_shift
    slope = slopecol_ref[...]

    def page_scores(n):
        k0 = k_refs[n][pl.ds(0, n_cols, stride=2), :].astype(BF16)
        k1 = k_refs[n][pl.ds(1, n_cols, stride=2), :].astype(BF16)
        return scores(k0, k1)

    ahead = 2
    pending = {n: page_scores(n) for n in range(min(ahead, pp))}
    for n in range(pp):
        if n + ahead < pp:
            pending[n + ahead] = page_scores(n + ahead)
        kpos = key + ((step * pp + n) * page - past_len)
        s = pending.pop(n) + slope * kpos.astype(F32)
        update(jnp.where(head_ok, s, MASK_VALUE), v_refs[n][...].astype(BF16))

    @pl.when(step == pl.num_programs(1) - 1)
    def _():
        ok, key_new, qi = geometry(kn_ref.shape[1])
        s = scores(kn_ref[0], kn_ref[1]) + slope * key_new.astype(F32)
        update(jnp.where(ok & (key_new <= qi), s, MASK_VALUE), vn_ref[...])
        lam = _lam_value(lamv_ref, lam_init)
        a = acc_ref[...] / l_ref[...]
        half = n_rows // 2
        o_ref[...] = _subln(a[:half] - lam * a[half:], g_ref[...], lam_init).astype(o_ref.dtype)


def _attn_sample(page_table, lamv, subln_g, slopecol, q2, k2d, v2d, k_new, v_new,
                 page, n_heads, n_new, lam_init):
    dec_batch, n_pages = page_table.shape
    dqk, dv = k2d.shape[1], v2d.shape[1]
    n_rows = 2 * n_heads * n_new
    pp = SAMPLE_PAGES
    assert n_pages % pp == 0
    pt_flat = page_table.reshape(-1)
    fixed = lambda b, s, pt: (0, 0)
    seq3 = lambda b, s, pt: (b, 0, 0)

    def page_map(n):
        return lambda b, s, pt: (pt[b * n_pages + s * pp + n], 0)

    kernel = functools.partial(
        _attn_sample_kernel, pp=pp, page=page, past_len=n_pages * page,
        n_heads=n_heads, n_new=n_new, lam_init=lam_init)
    grid_spec = pltpu.PrefetchScalarGridSpec(
        num_scalar_prefetch=1,
        grid=(dec_batch, n_pages // pp),
        in_specs=[pl.BlockSpec(lamv.shape, fixed),
                  pl.BlockSpec((1, dv), fixed),
                  pl.BlockSpec((n_rows, 1), fixed),
                  pl.BlockSpec((None,) + q2.shape[1:], lambda b, s, pt: (b, 0, 0, 0)),
                  pl.BlockSpec((None,) + k_new.shape[1:], lambda b, s, pt: (b, 0, 0, 0)),
                  pl.BlockSpec((None,) + v_new.shape[1:], seq3)]
        + [pl.BlockSpec((page * n_heads * 2, dqk), page_map(n)) for n in range(pp)]
        + [pl.BlockSpec((page * n_heads, dv), page_map(n)) for n in range(pp)],
        out_specs=pl.BlockSpec((None, n_heads * n_new, dv), seq3),
        scratch_shapes=[pltpu.VMEM((n_rows, dv), F32), pltpu.VMEM((n_rows, dv), F32),
                        pltpu.VMEM((n_rows, dv), F32)],
    )
    return pl.pallas_call(
        kernel,
        grid_spec=grid_spec,
        out_shape=jax.ShapeDtypeStruct((dec_batch, n_heads * n_new, dv), BF16),
        compiler_params=_params("arbitrary", "arbitrary"),
        name="attn_sample",
    )(pt_flat, lamv, subln_g, slopecol, q2, k_new, v_new, *([k2d] * pp), *([v2d] * pp))


def _ssm_prompt_kernel(u_ref, tm_ref, sm_ref, cm_ref, d_ref, coef_ref, y_ref, h_ref,
                       *, n_chunks, n_steps):
    u = u_ref[...]
    x = jnp.dot(u, sm_ref[...], precision=_HI, preferred_element_type=F32)
    half = x.shape[1] // 2
    x0, x1 = x[:, :half], x[:, half:]
    row = lax.broadcasted_iota(jnp.int32, x0.shape, 0)

    def shifted(a, d):
        return jnp.where(row >= d, pltpu.roll(a, d, 0), 0.0)

    y0, y1 = shifted(x0, 1), shifted(x1, 1)
    coef = coef_ref[...]
    for s in range(n_steps):
        d = 1 << s
        p = coef[2 * s:2 * s + 1]
        q = coef[2 * s + 1:2 * s + 2]
        s0, s1 = shifted(y0, d), shifted(y1, d)
        y0, y1 = y0 + p * s0 + q * s1, y1 + p * s1 - q * s0
    last = n_chunks - 1
    h_ref[...] = (coef[0:1] * y0[last:last + 1] + coef[1:2] * y1[last:last + 1]
                  + x0[last:last + 1])
    y = (jnp.dot(u, tm_ref[...], precision=_HI, preferred_element_type=F32)
         + jnp.dot(y0, cm_ref[...], precision=_HI, preferred_element_type=F32)
         + u * d_ref[...])
    y_ref[...] = _gelu(y)


def _ssm_prompt(u_r, tm, sm, cm, dsk, coef):
    batch, groups, n_chunks, width = u_r.shape
    n_steps = coef.shape[1] // 2
    state = cm.shape[1]
    return pl.pallas_call(
        functools.partial(_ssm_prompt_kernel, n_chunks=n_chunks, n_steps=n_steps),
        grid=(batch, groups),
        in_specs=[pl.BlockSpec((None, None, n_chunks, width), lambda b, g: (b, g, 0, 0)),
                  pl.BlockSpec((None, width, width), lambda b, g: (g, 0, 0)),
                  pl.BlockSpec((None, width, 2 * state), lambda b, g: (g, 0, 0)),
                  pl.BlockSpec((None, state, width), lambda b, g: (g, 0, 0)),
                  pl.BlockSpec((None, 1, width), lambda b, g: (g, 0, 0)),
                  pl.BlockSpec((None, 2 * n_steps, state), lambda b, g: (g, 0, 0))],
        out_specs=[pl.BlockSpec((None, None, n_chunks, width), lambda b, g: (b, g, 0, 0)),
                   pl.BlockSpec((None, None, 1, state), lambda b, g: (b, g, 0, 0))],
        out_shape=[jax.ShapeDtypeStruct((batch, groups, n_chunks, width), F32),
                   jax.ShapeDtypeStruct((batch, groups, 1, state), F32)],
        compiler_params=_params("arbitrary", "arbitrary"),
        name="ssm_prompt",
    )(u_r, tm, sm, cm, dsk, coef)


def _ssm_sample_kernel(u_ref, h0_ref, h0s_ref, tm_ref, sm_ref, cm_ref, d_ref, coef_ref,
                       y_ref, h_ref):
    u = u_ref[...]
    h0 = h0_ref[...]
    coef = coef_ref[...]
    x = jnp.dot(u, sm_ref[...], precision=_HI, preferred_element_type=F32)
    h_ref[...] = coef[0:1] * h0 + coef[1:2] * h0s_ref[...] + x
    y = (jnp.dot(u, tm_ref[...], precision=_HI, preferred_element_type=F32)
         + jnp.dot(h0, cm_ref[...], precision=_HI, preferred_element_type=F32)
         + u * d_ref[...])
    y_ref[...] = _gelu(y)


def _ssm_sample(u_r, h0, h0s, tm, sm, cm, dsk, coef):
    groups, seqs, width = u_r.shape
    state = h0.shape[2]
    return pl.pallas_call(
        _ssm_sample_kernel,
        grid=(groups,),
        in_specs=[pl.BlockSpec((None, seqs, width), lambda g: (g, 0, 0)),
                  pl.BlockSpec((None, seqs, state), lambda g: (g, 0, 0)),
                  pl.BlockSpec((None, seqs, state), lambda g: (g, 0, 0)),
                  pl.BlockSpec((None, width, width), lambda g: (g, 0, 0)),
                  pl.BlockSpec((None, width, state), lambda g: (g, 0, 0)),
                  pl.BlockSpec((None, state, width), lambda g: (g, 0, 0)),
                  pl.BlockSpec((None, 1, width), lambda g: (g, 0, 0)),
                  pl.BlockSpec((None, 2, state), lambda g: (g, 0, 0))],
        out_specs=[pl.BlockSpec((None, seqs, width), lambda g: (g, 0, 0)),
                   pl.BlockSpec((None, seqs, state), lambda g: (g, 0, 0))],
        out_shape=[jax.ShapeDtypeStruct((groups, seqs, width), F32),
                   jax.ShapeDtypeStruct((groups, seqs, state), F32)],
        compiler_params=_params("arbitrary"),
        name="ssm_sample",
    )(u_r, h0, h0s, tm, sm, cm, dsk, coef)


def _ssm_tables(lam_re, lam_im, log_dt, b_re, b_im, c_re, c_im, d_skip, chunk, n_steps):
    groups, p_dim, ch = b_re.shape
    lr, li = lam_re.astype(F32), lam_im.astype(F32)
    dt = jnp.exp(log_dt.astype(F32))[:, None]
    mag = jnp.exp(lr * dt)
    a_re, a_im = mag * jnp.cos(li * dt), mag * jnp.sin(li * dt)
    den = lr * lr + li * li
    n_re, n_im = a_re - 1.0, a_im
    g_re = (n_re * lr + n_im * li) / den
    g_im = (n_im * lr - n_re * li) / den
    br, bi = b_re.astype(F32), b_im.astype(F32)
    bb_re = g_re[..., None] * br - g_im[..., None] * bi
    bb_im = g_re[..., None] * bi + g_im[..., None] * br

    def power(n):
        n = n.astype(F32)[:, None, None]
        m = jnp.exp(lr * dt * n)
        return m * jnp.cos(li * dt * n), m * jnp.sin(li * dt * n)

    taus = jnp.arange(chunk + 1)
    pw_re, pw_im = power(taus)
    ab_re = pw_re[:chunk, :, :, None] * bb_re - pw_im[:chunk, :, :, None] * bb_im
    ab_im = pw_re[:chunk, :, :, None] * bb_im + pw_im[:chunk, :, :, None] * bb_re
    cr, ci = c_re.astype(F32), c_im.astype(F32)
    kern = (jnp.einsum('gcp,tgpd->gtcd', cr, ab_re, precision=_HI)
            - jnp.einsum('gcp,tgpd->gtcd', ci, ab_im, precision=_HI))
    s_idx = jnp.arange(chunk)[:, None]
    t_idx = jnp.arange(chunk)[None, :]
    lag = t_idx - s_idx
    tm = jnp.where((lag >= 0)[None, :, :, None, None], kern[:, jnp.maximum(lag, 0)], 0.0)
    tm = tm.transpose(0, 1, 4, 2, 3).reshape(groups, chunk * ch, chunk * ch)
    sm_re = ab_re[::-1].transpose(1, 0, 3, 2).reshape(groups, chunk * ch, p_dim)
    sm_im = ab_im[::-1].transpose(1, 0, 3, 2).reshape(groups, chunk * ch, p_dim)
    sm = jnp.concatenate([sm_re, sm_im, sm_im, sm_re], axis=-1)
    ca_re = cr[:, None] * pw_re[1:].transpose(1, 0, 2)[:, :, None, :] \
        - ci[:, None] * pw_im[1:].transpose(1, 0, 2)[:, :, None, :]
    ca_im = cr[:, None] * pw_im[1:].transpose(1, 0, 2)[:, :, None, :] \
        + ci[:, None] * pw_re[1:].transpose(1, 0, 2)[:, :, None, :]
    cm = jnp.concatenate([ca_re, -ca_im], axis=-1)
    cm = cm.transpose(0, 3, 1, 2).reshape(groups, 2 * p_dim, chunk * ch)
    dsk = jnp.tile(d_skip.astype(F32), (1, chunk)).reshape(groups, 1, chunk * ch)
    steps = chunk * (2 ** jnp.arange(n_steps))
    sp_re, sp_im = power(steps)
    p_rows = jnp.concatenate([sp_re, sp_re], axis=-1)
    q_rows = jnp.concatenate([-sp_im, sp_im], axis=-1)
    coef = jnp.stack([p_rows, q_rows], axis=1).reshape(2 * n_steps, groups, 2 * p_dim)
    return tm, sm, cm, dsk, coef.transpose(1, 0, 2)


def _layer_norm(y, g, b, axis):
    mu = jnp.mean(y, axis=axis, keepdims=True)
    var = jnp.mean(jnp.square(y - mu), axis=axis, keepdims=True)
    return (y - mu) * lax.rsqrt(var + EPS) * g + b


def _out_kernel(x_ref, a_ref, yg_ref, wglu_ref, bglu_ref, wo_ref, g_ref, b_ref, x1_ref,
                *, d_attn, alpha):
    yg = yg_ref[...]
    gate = jax.nn.sigmoid(
        jnp.dot(yg.astype(BF16), wglu_ref[...], preferred_element_type=F32) + bglu_ref[...])
    s = (yg * gate).astype(BF16)
    h = (jnp.dot(a_ref[...], wo_ref[:d_attn, :], preferred_element_type=F32)
         + jnp.dot(s, wo_ref[d_attn:, :], preferred_element_type=F32))
    x1_ref[...] = _layer_norm(alpha * x_ref[...] + h, g_ref[...], b_ref[...], -1)


def _out_proj(x_all, a_all, yg_all, wglu_bf, bglu, wo_bf, ln_g, ln_b, alpha):
    tp, d_model = x_all.shape
    d_attn = a_all.shape[1]
    d_ssm = yg_all.shape[1]
    tm = PROJ_TM
    row = lambda i: (i, 0)
    fixed = lambda i: (0, 0)
    return pl.pallas_call(
        functools.partial(_out_kernel, d_attn=d_attn, alpha=alpha),
        grid=(tp // tm,),
        in_specs=[pl.BlockSpec((tm, d_model), row), pl.BlockSpec((tm, d_attn), row),
                  pl.BlockSpec((tm, d_ssm), row), pl.BlockSpec((d_ssm, d_ssm), fixed),
                  pl.BlockSpec((1, d_ssm), fixed), pl.BlockSpec((d_model, d_model), fixed),
                  pl.BlockSpec((1, d_model), fixed), pl.BlockSpec((1, d_model), fixed)],
        out_specs=pl.BlockSpec((tm, d_model), row),
        out_shape=jax.ShapeDtypeStruct((tp, d_model), F32),
        compiler_params=_params("arbitrary"),
        name="out_proj",
    )(x_all, a_all, yg_all, wglu_bf, bglu, wo_bf, ln_g, ln_b)


def _top_k_rows(s, k):
    n = s.shape[0]
    iota = lax.broadcasted_iota(jnp.int32, s.shape, 0).astype(F32)
    rank = jnp.full(s.shape, float(k), F32)
    vals, idxs = [], []
    for r in range(k):
        mx = jnp.max(s, axis=0, keepdims=True)
        ix = jnp.min(jnp.where(s == mx, iota, float(n)), axis=0, keepdims=True)
        hit = iota == ix
        rank = jnp.where(hit, float(r), rank)
        s = jnp.where(hit, NEG_INF, s)
        vals.append(mx)
        idxs.append(ix)
    return vals, idxs, rank


def _staircase(k):
    return [(a, b) for a in range(k) for b in range(k) if (a + 1) * (b + 1) <= k]


def _peer_sel_kernel(x1_ref, wq_ref, qg_ref, keys_ref, n1_ref, e1_ref, r2_ref, e2_ref,
                     *, n_heads, d_key, topk):
    qp = jnp.dot(x1_ref[...].astype(BF16), wq_ref[...], preferred_element_type=F32)
    d_half = d_key // 2
    pairs = _staircase(topk)
    n_pairs = len(pairs)
    n_rows = -(-n_pairs // 8) * 8
    tb = qp.shape[0]
    for h in range(n_heads):
        qh = qp[:, h * d_key:(h + 1) * d_key]
        qn = qh * lax.rsqrt(jnp.mean(qh * qh, axis=-1, keepdims=True) + EPS) * qg_ref[...]
        qb = qn.astype(BF16)
        s1 = lax.dot_general(keys_ref[2 * h], qb[:, :d_half], _NT, preferred_element_type=F32)
        s2 = lax.dot_general(keys_ref[2 * h + 1], qb[:, d_half:], _NT, preferred_element_type=F32)
        t1, i1, _ = _top_k_rows(s1, topk)
        t2, _, r2 = _top_k_rows(s2, topk)
        cand = jnp.concatenate(
            [t1[a] + t2[b] for a, b in pairs]
            + [jnp.full((n_rows - n_pairs, tb), NEG_INF, F32)], axis=0)
        iota = lax.broadcasted_iota(jnp.int32, cand.shape, 0).astype(F32)
        work = cand
        z = jnp.zeros((1, tb), F32)
        m = t1[0] + t2[0]
        for _ in range(topk):
            mx = jnp.max(work, axis=0, keepdims=True)
            ix = jnp.min(jnp.where(work == mx, iota, float(n_rows)), axis=0, keepdims=True)
            work = jnp.where(iota == ix, NEG_INF, work)
            z = z + jnp.exp(mx - m)
        chosen = jnp.where((work == NEG_INF) & (iota < float(n_pairs)), 1.0, 0.0)
        key_iota = lax.broadcasted_iota(jnp.int32, s1.shape, 0).astype(F32)
        n1 = jnp.zeros(s1.shape, F32)
        off = 0
        for a in range(topk):
            cnt = sum(1 for pa, _ in pairs if pa == a)
            n_a = jnp.sum(chosen[off:off + cnt], axis=0, keepdims=True)
            off += cnt
            n1 = jnp.where(key_iota == i1[a], n_a, n1)
        n1_ref[h] = n1.astype(n1_ref.dtype)
        e1_ref[h] = jnp.exp(s1 - t1[0]).astype(e1_ref.dtype)
        r2_ref[h] = pltpu.bitcast(r2.astype(BF16), r2_ref.dtype)
        e2_ref[h] = pltpu.bitcast((jnp.exp(s2 - t2[0]) / z).astype(BF16), e2_ref.dtype)


def _peer_select(x1, wq_bf, q_g, keys_bf, n_heads, d_key, n_keys):
    tp, d_model = x1.shape
    tb = SEL_TB
    row_spec = pl.BlockSpec((n_heads, n_keys, tb), lambda i: (0, 0, i))
    tile_spec = pl.BlockSpec((n_heads, n_keys // 2, tb), lambda i: (0, 0, i))
    out_shape = [jax.ShapeDtypeStruct((n_heads, n_keys, tp), F32)] * 2 + [
        jax.ShapeDtypeStruct((n_heads, n_keys // 2, tp), jnp.uint32)] * 2
    return pl.pallas_call(
        functools.partial(_peer_sel_kernel, n_heads=n_heads, d_key=d_key, topk=PEER_TOPK),
        grid=(tp // tb,),
        in_specs=[pl.BlockSpec((tb, d_model), lambda i: (i, 0)),
                  pl.BlockSpec(wq_bf.shape, lambda i: (0, 0)),
                  pl.BlockSpec((1, d_key), lambda i: (0, 0)),
                  pl.BlockSpec(keys_bf.shape, lambda i: (0, 0, 0))],
        out_specs=[row_spec, row_spec, tile_spec, tile_spec],
        out_shape=out_shape,
        compiler_params=_params("arbitrary"),
        name="peer_select",
    )(x1, wq_bf, q_g, keys_bf)


def _peer_dense_kernel(xt_ref, xtb_ref, u_ref, vt_ref, n1_ref, e1_ref, r2_ref, e2_ref,
                       g_ref, b_ref, o_ref, acc_ref, act_ref, wg_ref,
                       *, n_heads, n_keys, lane_tile, alpha):
    c = pl.program_id(1)
    n_chunks = pl.num_programs(1) - 1
    n_i = act_ref.shape[0] // n_keys
    tb = act_ref.shape[1]
    zero = jnp.zeros((n_keys, lane_tile), BF16)

    @pl.when(c == 0)
    def _():
        acc_ref[...] = jnp.zeros(acc_ref.shape, F32)
        wg_ref[1] = jnp.zeros(wg_ref.shape[1:], BF16)

    def row_tile(ref, h, il, cols):
        row = jnp.broadcast_to(ref[h, il:il + 1, cols], (BF16_SUBLANES, lane_tile)).astype(BF16)
        return jnp.tile(row, (n_keys // BF16_SUBLANES, 1))

    def pv_previous():
        acc_ref[...] += jnp.dot(vt_ref[...], wg_ref[(c + 1) % 2], preferred_element_type=F32)

    @pl.when(c < n_chunks)
    def _():
        pv_previous()
        act_ref[...] = jnp.dot(u_ref[...], xtb_ref[...], preferred_element_type=F32)
        cur = c % 2
        for tc in range(tb // lane_tile):
            cols = slice(tc * lane_tile, (tc + 1) * lane_tile)
            for il in range(n_i):
                w = zero
                for h in range(n_heads):
                    sel = pltpu.bitcast(r2_ref[h, :, cols], BF16) < row_tile(n1_ref, h, il, cols)
                    e2 = pltpu.bitcast(e2_ref[h, :, cols], BF16)
                    w = w + jnp.where(sel, e2, zero) * row_tile(e1_ref, h, il, cols)
                rows = slice(il * n_keys, (il + 1) * n_keys)
                wg_ref[cur, rows, cols] = w * _gelu(act_ref[rows, cols]).astype(BF16)

    @pl.when(c == n_chunks)
    def _():
        pv_previous()
        o_ref[...] = _layer_norm(alpha * xt_ref[...] + acc_ref[...], g_ref[...], b_ref[...], 0)


def _peer_dense(x1t, x1t_bf, u_bf, vt_bf, n1, e1, r2, e2, ln_g, ln_b, alpha):
    d_model, tp = x1t.shape
    n_heads, n_keys, _ = n1.shape
    n_exp = u_bf.shape[0]
    tb, ec = PEER_TB, PEER_EC
    n_i = ec // n_keys
    n_chunks = n_exp // ec
    tok = lambda t, c: (0, t)
    cur = lambda c: jnp.minimum(c, n_chunks - 1)
    prev = lambda c: jnp.maximum(c - 1, 0)
    return pl.pallas_call(
        functools.partial(_peer_dense_kernel, n_heads=n_heads, n_keys=n_keys,
                          lane_tile=PEER_LANE_TILE, alpha=alpha),
        grid=(tp // tb, n_chunks + 1),
        in_specs=[pl.BlockSpec((d_model, tb), tok),
                  pl.BlockSpec((d_model, tb), tok),
                  pl.BlockSpec((ec, d_model), lambda t, c: (cur(c), 0)),
                  pl.BlockSpec((d_model, ec), lambda t, c: (0, prev(c))),
                  pl.BlockSpec((n_heads, n_i, tb), lambda t, c: (0, cur(c), t)),
                  pl.BlockSpec((n_heads, n_i, tb), lambda t, c: (0, cur(c), t)),
                  pl.BlockSpec((n_heads, n_keys // 2, tb), lambda t, c: (0, 0, t)),
                  pl.BlockSpec((n_heads, n_keys // 2, tb), lambda t, c: (0, 0, t)),
                  pl.BlockSpec((d_model, 1), lambda t, c: (0, 0)),
                  pl.BlockSpec((d_model, 1), lambda t, c: (0, 0))],
        out_specs=pl.BlockSpec((d_model, tb), tok),
        out_shape=jax.ShapeDtypeStruct((d_model, tp), F32),
        scratch_shapes=[pltpu.VMEM((d_model, tb), F32), pltpu.VMEM((ec, tb), F32),
                        pltpu.VMEM((2, ec, tb), BF16)],
        compiler_params=_params("arbitrary", "arbitrary"),
        name="peer_dense",
    )(x1t, x1t_bf, u_bf, vt_bf, n1, e1, r2, e2, ln_g, ln_b)


def kernel(x_prompt, x_sample, cache_k, cache_v, state_ssm_re, state_ssm_im, page_table,
           w_in, lambda_q1, lambda_k1, lambda_q2, lambda_k2, attn_subln_g,
           ssm_lambda_re, ssm_lambda_im, ssm_log_dt, ssm_b_re, ssm_b_im, ssm_c_re, ssm_c_im,
           ssm_d, ssm_w_glu, ssm_b_glu, w_out, ln1_g, ln1_b,
           peer_w_q, peer_q_g, peer_sub_keys, peer_u, peer_v, ln2_g, ln2_b):
    depth = w_in.shape[0]
    batch, seq, d_model = x_prompt.shape
    dec_batch, dec_seq, _ = x_sample.shape
    _, n_phys, page, n_heads, _, dqk = cache_k.shape
    dv = cache_v.shape[-1]
    d_attn = n_heads * dv
    groups, p_dim = state_ssm_re.shape[2:]
    ch = ssm_b_re.shape[-1]
    peer_heads, _, n_keys, d_half = peer_sub_keys.shape[1:]
    d_key = 2 * d_half
    alpha = (2.0 * depth) ** 0.25
    scale = dqk ** -0.5

    n_prompt = batch * seq
    n_sample = dec_batch * dec_seq
    n_tok = n_prompt + n_sample
    tp = -(-n_tok // TOKEN_PAD) * TOKEN_PAD
    n_chunks = seq // SSM_CHUNK
    n_steps = int(math.log2(n_chunks))
    assert n_chunks == 1 << n_steps and seq % ATTN_TQ == 0 and n_prompt % TOKEN_PAD == 0

    x_all = jnp.concatenate(
        [x_prompt.reshape(n_prompt, d_model), x_sample.reshape(n_sample, d_model),
         jnp.zeros((tp - n_tok, d_model), x_prompt.dtype)], axis=0)
    slope_vals = 2.0 ** (-8.0 * jnp.arange(1, n_heads + 1, dtype=F32) / n_heads)
    slopes = jnp.broadcast_to(slope_vals[:, None, None], (n_heads, 1, 128))
    slopecol = jnp.tile(jnp.repeat(slope_vals, dec_seq), 2)[:, None]

    outs = {name: [] for name in ("kp", "vp", "hpr", "hpi", "ks", "vs", "hsr", "hsi")}
    for l in range(depth):
        lam_init = 0.8 - 0.6 * math.exp(-0.3 * l)
        lamv = jnp.stack([lambda_q1[l], lambda_k1[l], lambda_q2[l], lambda_k2[l]]).astype(F32)
        subln_g = attn_subln_g[l].astype(F32)[None, :]

        k_all, v_all, u_all, qkv_bf = _project(x_all, w_in[l].astype(BF16), d_attn, scale)

        a_prompt = _attn_prompt(qkv_bf, lamv, subln_g, slopes, batch, seq, n_heads, dv, dqk,
                                lam_init)

        qkv_s = qkv_bf[n_prompt:n_tok].reshape(dec_batch, dec_seq, 3 * d_attn)
        q2 = (qkv_s[..., :d_attn].reshape(dec_batch, dec_seq, n_heads, 2, dqk)
              .transpose(0, 3, 2, 1, 4).reshape(dec_batch, 2, n_heads * dec_seq, dqk))
        new_rows = dec_seq * n_heads
        k_new = (qkv_s[..., d_attn:2 * d_attn].reshape(dec_batch, new_rows, 2, dqk)
                 .transpose(0, 2, 1, 3))
        k_new = jnp.pad(k_new, ((0, 0), (0, 0), (0, 128 - new_rows), (0, 0)))
        v_new = jnp.pad(qkv_s[..., 2 * d_attn:].reshape(dec_batch, new_rows, dv),
                        ((0, 0), (0, 128 - new_rows), (0, 0)))
        a_s = _attn_sample(page_table, lamv, subln_g, slopecol, q2,
                           cache_k[l].reshape(n_phys * page * n_heads * 2, dqk),
                           cache_v[l].reshape(n_phys * page * n_heads, dv),
                           k_new, v_new, page, n_heads, dec_seq, lam_init)
        a_sample = a_s.reshape(dec_batch, n_heads, dec_seq, dv).transpose(0, 2, 1, 3)
        a_all = jnp.concatenate(
            [a_prompt, a_sample.reshape(n_sample, d_attn),
             jnp.zeros((tp - n_tok, d_attn), BF16)], axis=0)

        ssm_p = (ssm_lambda_re[l], ssm_lambda_im[l], ssm_log_dt[l], ssm_b_re[l], ssm_b_im[l],
                 ssm_c_re[l], ssm_c_im[l], ssm_d[l])
        tm, sm, cm, dsk, coef = _ssm_tables(*ssm_p, SSM_CHUNK, n_steps)
        u_r = (u_all[:n_prompt].reshape(batch, n_chunks, SSM_CHUNK, groups, ch)
               .transpose(0, 3, 1, 2, 4).reshape(batch, groups, n_chunks, SSM_CHUNK * ch))
        y_r, h_p = _ssm_prompt(u_r, tm, sm, cm, dsk, coef)
        yg_prompt = (y_r.reshape(batch, groups, n_chunks, SSM_CHUNK, ch)
                     .transpose(0, 2, 3, 1, 4).reshape(n_prompt, groups * ch))

        tm_s, sm_s, cm_s, dsk_s, coef_s = _ssm_tables(*ssm_p, dec_seq, 1)
        us_r = (u_all[n_prompt:n_tok].reshape(dec_batch, dec_seq, groups, ch)
                .transpose(2, 0, 1, 3).reshape(groups, dec_batch, dec_seq * ch))
        h0r = state_ssm_re[l].astype(F32).transpose(1, 0, 2)
        h0i = state_ssm_im[l].astype(F32).transpose(1, 0, 2)
        ys_r, h_s = _ssm_sample(us_r, jnp.concatenate([h0r, h0i], -1),
                                jnp.concatenate([h0i, h0r], -1),
                                tm_s, sm_s[..., :2 * p_dim], cm_s, dsk_s, coef_s)
        yg_sample = (ys_r.reshape(groups, dec_batch, dec_seq, ch)
                     .transpose(1, 2, 0, 3).reshape(n_sample, groups * ch))
        yg_all = jnp.concatenate(
            [yg_prompt, yg_sample, jnp.zeros((tp - n_tok, groups * ch), F32)], axis=0)

        x1 = _out_proj(x_all, a_all, yg_all, ssm_w_glu[l].astype(BF16),
                       ssm_b_glu[l].astype(F32)[None, :], w_out[l].astype(BF16),
                       ln1_g[l].astype(F32)[None, :], ln1_b[l].astype(F32)[None, :], alpha)

        keys_bf = peer_sub_keys[l].reshape(2 * peer_heads, n_keys, d_half).astype(BF16)
        n1, e1, r2, e2 = _peer_select(x1, peer_w_q[l].astype(BF16),
                                      peer_q_g[l].astype(F32)[None, :], keys_bf,
                                      peer_heads, d_key, n_keys)
        x1t = x1.T
        out_t = _peer_dense(x1t, x1t.astype(BF16), peer_u[l].astype(BF16),
                            peer_v[l].T.astype(BF16), n1, e1, r2, e2,
                            ln2_g[l].astype(F32)[:, None], ln2_b[l].astype(F32)[:, None], alpha)
        x_all = out_t.T

        outs["kp"].append(k_all[:n_prompt].reshape(batch, seq, n_heads, 2, dqk))
        outs["vp"].append(v_all[:n_prompt].reshape(batch, seq, n_heads, dv))
        outs["hpr"].append(h_p[:, :, 0, :p_dim])
        outs["hpi"].append(h_p[:, :, 0, p_dim:])
        outs["ks"].append(k_all[n_prompt:n_tok].reshape(dec_batch, dec_seq, n_heads, 2, dqk))
        outs["vs"].append(v_all[n_prompt:n_tok].reshape(dec_batch, dec_seq, n_heads, dv))
        outs["hsr"].append(h_s[:, :, :p_dim].transpose(1, 0, 2))
        outs["hsi"].append(h_s[:, :, p_dim:].transpose(1, 0, 2))

    y_prompt = x_all[:n_prompt].reshape(batch, seq, d_model)
    y_sample = x_all[n_prompt:n_tok].reshape(dec_batch, dec_seq, d_model)
    st = lambda name: jnp.stack(outs[name])
    return (y_prompt, y_sample, st("kp"), st("vp"), st("hpr"), st("hpi"),
            st("ks"), st("vs"), st("hsr"), st("hsi"))
```

```python
import functools
import math

import jax
import jax.numpy as jnp
from jax import lax
from jax.experimental import pallas as pl
from jax.experimental.pallas import tpu as pltpu

F32 = jnp.float32
BF16 = jnp.bfloat16

EPS = 1e-5
MASK_VALUE = -1e30
PEER_TOPK = 16
NEG_INF = float("-inf")

TOKEN_PAD = 512
PROJ_TM = 512
ATTN_TQ = 512
ATTN_RS = 128
SAMPLE_PAGES = 8
SSM_CHUNK = 16
SEL_TB = 256
PEER_TB = 512
PEER_EC = 1024
BF16_SUBLANES = 16
PEER_LANE_TILE = 128
VMEM_LIMIT = 56 * 1024 * 1024

_NT = (((1,), (1,)), ((), ()))
_HI = lax.Precision.HIGHEST


def _params(*sem):
    return pltpu.CompilerParams(dimension_semantics=sem, vmem_limit_bytes=VMEM_LIMIT)


def _gelu(x):
    return 0.5 * x * (1.0 + lax.erf(x * (2.0 ** -0.5)))


def _proj_kernel(x_ref, w_ref, wkt_ref, kt_ref, ks_ref, v_ref, u_ref, qkv_ref,
                 *, d_attn, scale, n_prompt_blocks):
    i = pl.program_id(0)
    x = x_ref[...].astype(BF16)
    z = jnp.dot(x, w_ref[...], preferred_element_type=F32)
    k = z[:, d_attn:2 * d_attn]
    v = z[:, 2 * d_attn:3 * d_attn]
    v_ref[...] = v
    u_ref[...] = z[:, 3 * d_attn:]
    qkv_ref[:, :d_attn] = (z[:, :d_attn] * scale).astype(BF16)
    qkv_ref[:, d_attn:2 * d_attn] = k.astype(BF16)
    qkv_ref[:, 2 * d_attn:] = v.astype(BF16)

    @pl.when(i < n_prompt_blocks)
    def _():
        kt_ref[...] = lax.dot_general(wkt_ref[...], x, _NT, preferred_element_type=F32)

    @pl.when(i == pl.num_programs(0) - 1)
    def _():
        ks_ref[...] = k


def _project(x_all, w_bf, wkt_bf, d_attn, scale, batch, seq):
    tp, d_model = x_all.shape
    n_out = w_bf.shape[1]
    d_u = n_out - 3 * d_attn
    tm = PROJ_TM
    per_seq = seq // tm
    n_prompt_blocks = batch * per_seq
    assert tp // tm == n_prompt_blocks + 1
    last = n_prompt_blocks - 1
    row = lambda i: (i, 0)
    kt_map = lambda i: (jnp.minimum(i, last) // per_seq, 0, jnp.minimum(i, last) % per_seq)
    return pl.pallas_call(
        functools.partial(_proj_kernel, d_attn=d_attn, scale=scale,
                          n_prompt_blocks=n_prompt_blocks),
        grid=(tp // tm,),
        in_specs=[pl.BlockSpec((tm, d_model), row),
                  pl.BlockSpec((d_model, n_out), lambda i: (0, 0)),
                  pl.BlockSpec((d_attn, d_model), lambda i: (0, 0))],
        out_specs=[pl.BlockSpec((None, d_attn, tm), kt_map),
                   pl.BlockSpec((tm, d_attn), lambda i: (0, 0)),
                   pl.BlockSpec((tm, d_attn), row),
                   pl.BlockSpec((tm, d_u), row),
                   pl.BlockSpec((tm, 3 * d_attn), row)],
        out_shape=[jax.ShapeDtypeStruct((batch, d_attn, seq), F32),
                   jax.ShapeDtypeStruct((tm, d_attn), F32),
                   jax.ShapeDtypeStruct((tp, d_attn), F32),
                   jax.ShapeDtypeStruct((tp, d_u), F32),
                   jax.ShapeDtypeStruct((tp, 3 * d_attn), BF16)],
        compiler_params=_params("arbitrary"),
        name="proj",
    )(x_all, w_bf, wkt_bf)


def _lam_value(lamv_ref, lam_init):
    lv = lamv_ref[...]
    e1 = jnp.exp(jnp.sum(lv[0:1] * lv[1:2], axis=1, keepdims=True))
    e2 = jnp.exp(jnp.sum(lv[2:3] * lv[3:4], axis=1, keepdims=True))
    return e1 - e2 + lam_init


def _subln(o, g, lam_init):
    ms = jnp.mean(o * o, axis=-1, keepdims=True)
    return o * lax.rsqrt(ms + EPS) * g * (1.0 - lam_init)


def _attn_prompt_kernel(qi_ref, kj_ref, lamv_ref, g_ref, slope_ref, q_ref, k_ref, v_ref, o_ref,
                        m_ref, l_ref, acc_ref, *, tq, rs, dqk, lam_init):
    t = pl.program_id(2)
    i = qi_ref[t]
    j = kj_ref[t]
    lanes = m_ref.shape[-1]

    @pl.when(j == 0)
    def _():
        m_ref[...] = jnp.full(m_ref.shape, MASK_VALUE, F32)
        l_ref[...] = jnp.zeros(l_ref.shape, F32)
        acc_ref[...] = jnp.zeros(acc_ref.shape, F32)

    n_sub = tq // rs

    def step(diagonal):
        kcol = lax.broadcasted_iota(jnp.int32, (1, tq), 1) + (j - i) * tq
        colbias = slope_ref[...][:, :1] * kcol.astype(F32)

        def n_keys(r):
            return (r + 1) * rs if diagonal else tq

        def scores(r):
            q = q_ref[r * rs:(r + 1) * rs, :]
            lane = lax.broadcasted_iota(jnp.int32, q.shape, 1)
            zero = jnp.zeros_like(q)
            q2 = jnp.concatenate([jnp.where(lane < dqk, q, zero), jnp.where(lane < dqk, zero, q)], axis=0)
            return lax.dot_general(q2, k_ref[:n_keys(r), :], _NT, preferred_element_type=F32)

        ahead = 2
        pending = {r: scores(r) for r in range(min(ahead, n_sub))}
        for r in range(n_sub):
            if r + ahead < n_sub:
                pending[r + ahead] = scores(r + ahead)
            nk = n_keys(r)
            rows = slice(2 * r * rs, 2 * (r + 1) * rs)
            s = pending.pop(r) + colbias[:, :nk]
            if diagonal:
                row = lax.broadcasted_iota(jnp.int32, (2 * rs, nk), 0) & (rs - 1)
                col = lax.broadcasted_iota(jnp.int32, (2 * rs, nk), 1)
                s = jnp.where(row + r * rs >= col, s, MASK_VALUE)
            m_prev = m_ref[rows, :]
            m_new = jnp.maximum(m_prev, jnp.max(s, axis=1, keepdims=True))
            alpha = jnp.exp(m_prev - m_new)
            p = jnp.exp(s - jnp.tile(m_new, (1, nk // lanes)))
            l_ref[rows, :] = alpha * l_ref[rows, :] + jnp.sum(p, axis=1, keepdims=True)
            acc_ref[rows, :] = alpha * acc_ref[rows, :] + jnp.dot(
                p.astype(BF16), v_ref[:nk, :], preferred_element_type=F32)
            m_ref[rows, :] = m_new

    @pl.when(j < i)
    def _():
        step(False)

    @pl.when(j == i)
    def _():
        step(True)
        lam = _lam_value(lamv_ref, lam_init)
        for r in range(n_sub):
            r1 = slice(2 * r * rs, (2 * r + 1) * rs)
            r2 = slice((2 * r + 1) * rs, 2 * (r + 1) * rs)
            o = acc_ref[r1, :] / l_ref[r1, :] - lam * (acc_ref[r2, :] / l_ref[r2, :])
            o_ref[r * rs:(r + 1) * rs, :] = _subln(o, g_ref[...], lam_init).astype(o_ref.dtype)


def _attn_prompt(qkv_bf, lamv, subln_g, slopes, batch, seq, n_heads, dv, dqk, lam_init):
    tq = ATTN_TQ
    nq = seq // tq
    d_attn = n_heads * dv
    pairs = [(i, j) for i in range(nq) for j in range(i + 1)]
    qi = jnp.asarray([p[0] for p in pairs], jnp.int32)
    kj = jnp.asarray([p[1] for p in pairs], jnp.int32)
    grid_spec = pltpu.PrefetchScalarGridSpec(
        num_scalar_prefetch=2,
        grid=(batch, n_heads, len(pairs)),
        in_specs=[pl.BlockSpec(lamv.shape, lambda b, h, t, qi, kj: (0, 0)),
                  pl.BlockSpec((1, dv), lambda b, h, t, qi, kj: (0, 0)),
                  pl.BlockSpec((None, 1, 128), lambda b, h, t, qi, kj: (h, 0, 0)),
                  pl.BlockSpec((tq, dv), lambda b, h, t, qi, kj: (b * nq + qi[t], h)),
                  pl.BlockSpec((tq, dv), lambda b, h, t, qi, kj: (b * nq + kj[t], n_heads + h)),
                  pl.BlockSpec((tq, dv), lambda b, h, t, qi, kj: (b * nq + kj[t], 2 * n_heads + h))],
        out_specs=pl.BlockSpec((tq, dv), lambda b, h, t, qi, kj: (b * nq + qi[t], h)),
        scratch_shapes=[pltpu.VMEM((2 * tq, dv), F32), pltpu.VMEM((2 * tq, dv), F32),
                        pltpu.VMEM((2 * tq, dv), F32)],
    )
    return pl.pallas_call(
        functools.partial(_attn_prompt_kernel, tq=tq, rs=ATTN_RS, dqk=dqk, lam_init=lam_init),
        grid_spec=grid_spec,
        out_shape=jax.ShapeDtypeStruct((batch * seq, d_attn), BF16),
        compiler_params=_params("arbitrary", "arbitrary", "arbitrary"),
        name="attn_prompt",
    )(qi, kj, lamv, subln_g, slopes, qkv_bf, qkv_bf, qkv_bf)


def _attn_sample_kernel(pt_ref, lamv_ref, g_ref, slopecol_ref, qbd_ref, kn_ref, vn_ref, *rest,
                        pp, page, past_len, n_heads, n_new, lam_init):
    del pt_ref
    k_refs, v_refs = rest[:pp], rest[pp:2 * pp]
    o_ref, m_ref, l_ref, acc_ref = rest[2 * pp:]
    step = pl.program_id(1)
    rows_h = 2 * n_new
    dv = acc_ref.shape[1]

    @pl.when(step == 0)
    def _():
        m_ref[...] = jnp.full(m_ref.shape, MASK_VALUE, F32)
        l_ref[...] = jnp.zeros(l_ref.shape, F32)
        acc_ref[...] = jnp.zeros(acc_ref.shape, F32)

    def update(s, v_of_head):
        m_prev = m_ref[...]
        m_new = jnp.maximum(m_prev, jnp.max(s, axis=1, keepdims=True))
        alpha = jnp.exp(m_prev - m_new)
        p = jnp.exp(s - m_new)
        l_ref[...] = alpha * l_ref[...] + jnp.sum(p, axis=1, keepdims=True)
        pb = p.astype(BF16)
        for h in range(n_heads):
            r = slice(h * rows_h, (h + 1) * rows_h)
            acc_ref[r, :] = alpha[r] * acc_ref[r, :] + jnp.dot(
                pb[r], v_of_head(h), preferred_element_type=F32)
        m_ref[...] = m_new

    slope = slopecol_ref[...]
    key = lax.broadcasted_iota(jnp.int32, (1, page), 1)

    def page_scores(n):
        return jnp.dot(qbd_ref[...], k_refs[n][...].astype(BF16), preferred_element_type=F32)

    ahead = 2
    pending = {n: page_scores(n) for n in range(min(ahead, pp))}
    for n in range(pp):
        if n + ahead < pp:
            pending[n + ahead] = page_scores(n + ahead)
        kpos = key + ((step * pp + n) * page - past_len)
        s = pending.pop(n) + slope * kpos.astype(F32)
        update(s, lambda h, n=n: v_refs[n][pl.ds(h, page, stride=n_heads), :].astype(BF16))

    @pl.when(step == pl.num_programs(1) - 1)
    def _():
        s = jnp.dot(qbd_ref[...], kn_ref[...], preferred_element_type=F32)
        col = lax.broadcasted_iota(jnp.int32, s.shape, 1)
        qi = lax.broadcasted_iota(jnp.int32, s.shape, 0) & (n_new - 1)
        s = s + slope * col.astype(F32)
        update(jnp.where(col <= qi, s, MASK_VALUE), lambda h: vn_ref[:, h * dv:(h + 1) * dv])
        lam = _lam_value(lamv_ref, lam_init)
        a = acc_ref[...] / l_ref[...]
        for h in range(n_heads):
            o = a[h * rows_h:h * rows_h + n_new] - lam * a[h * rows_h + n_new:(h + 1) * rows_h]
            o_ref[h * n_new:(h + 1) * n_new, :] = _subln(o, g_ref[...], lam_init).astype(o_ref.dtype)


def _attn_sample(page_table, lamv, subln_g, slopecol, qbd, kt2d, v2d, k_new, v_new,
                 n_heads, n_new, lam_init):
    dec_batch, n_pages = page_table.shape
    page, dv = kt2d.shape[1], v2d.shape[1]
    n_rows, d_attn = qbd.shape[1:]
    pp = SAMPLE_PAGES
    assert n_pages % pp == 0
    pt_flat = page_table.reshape(-1)
    fixed = lambda b, s, pt: (0, 0)
    seq3 = lambda b, s, pt: (b, 0, 0)

    def page_map(n):
        return lambda b, s, pt: (pt[b * n_pages + s * pp + n], 0)

    kernel = functools.partial(
        _attn_sample_kernel, pp=pp, page=page, past_len=n_pages * page,
        n_heads=n_heads, n_new=n_new, lam_init=lam_init)
    grid_spec = pltpu.PrefetchScalarGridSpec(
        num_scalar_prefetch=1,
        grid=(dec_batch, n_pages // pp),
        in_specs=[pl.BlockSpec(lamv.shape, fixed),
                  pl.BlockSpec((1, dv), fixed),
                  pl.BlockSpec((n_rows, 1), fixed),
                  pl.BlockSpec((None, n_rows, d_attn), seq3),
                  pl.BlockSpec((None,) + k_new.shape[1:], seq3),
                  pl.BlockSpec((None,) + v_new.shape[1:], seq3)]
        + [pl.BlockSpec((d_attn, page), page_map(n)) for n in range(pp)]
        + [pl.BlockSpec((page * n_heads, dv), page_map(n)) for n in range(pp)],
        out_specs=pl.BlockSpec((None, n_heads * n_new, dv), seq3),
        scratch_shapes=[pltpu.VMEM((n_rows, dv), F32), pltpu.VMEM((n_rows, dv), F32),
                        pltpu.VMEM((n_rows, dv), F32)],
    )
    return pl.pallas_call(
        kernel,
        grid_spec=grid_spec,
        out_shape=jax.ShapeDtypeStruct((dec_batch, n_heads * n_new, dv), BF16),
        compiler_params=_params("arbitrary", "arbitrary"),
        name="attn_sample",
    )(pt_flat, lamv, subln_g, slopecol, qbd, k_new, v_new, *([kt2d] * pp), *([v2d] * pp))


def _ssm_prompt_kernel(u_ref, tm_ref, sm_ref, cm_ref, d_ref, coef_ref, y_ref, h_ref,
                       *, n_chunks, n_steps):
    u = u_ref[...]
    x = jnp.dot(u, sm_ref[...], precision=_HI, preferred_element_type=F32)
    half = x.shape[1] // 2
    x0, x1 = x[:, :half], x[:, half:]
    row = lax.broadcasted_iota(jnp.int32, x0.shape, 0)

    def shifted(a, d):
        return jnp.where(row >= d, pltpu.roll(a, d, 0), 0.0)

    y0, y1 = shifted(x0, 1), shifted(x1, 1)
    coef = coef_ref[...]
    for s in range(n_steps):
        d = 1 << s
        p = coef[2 * s:2 * s + 1]
        q = coef[2 * s + 1:2 * s + 2]
        s0, s1 = shifted(y0, d), shifted(y1, d)
        y0, y1 = y0 + p * s0 + q * s1, y1 + p * s1 - q * s0
    last = n_chunks - 1
    h_ref[...] = (coef[0:1] * y0[last:last + 1] + coef[1:2] * y1[last:last + 1]
                  + x0[last:last + 1])
    y = (jnp.dot(u, tm_ref[...], precision=_HI, preferred_element_type=F32)
         + jnp.dot(y0, cm_ref[...], precision=_HI, preferred_element_type=F32)
         + u * d_ref[...])
    y_ref[...] = _gelu(y)


def _ssm_prompt(u_r, tm, sm, cm, dsk, coef):
    batch, groups, n_chunks, width = u_r.shape
    n_steps = coef.shape[1] // 2
    state = cm.shape[1]
    return pl.pallas_call(
        functools.partial(_ssm_prompt_kernel, n_chunks=n_chunks, n_steps=n_steps),
        grid=(batch, groups),
        in_specs=[pl.BlockSpec((None, None, n_chunks, width), lambda b, g: (b, g, 0, 0)),
                  pl.BlockSpec((None, width, width), lambda b, g: (g, 0, 0)),
                  pl.BlockSpec((None, width, 2 * state), lambda b, g: (g, 0, 0)),
                  pl.BlockSpec((None, state, width), lambda b, g: (g, 0, 0)),
                  pl.BlockSpec((None, 1, width), lambda b, g: (g, 0, 0)),
                  pl.BlockSpec((None, 2 * n_steps, state), lambda b, g: (g, 0, 0))],
        out_specs=[pl.BlockSpec((None, None, n_chunks, width), lambda b, g: (b, g, 0, 0)),
                   pl.BlockSpec((None, None, 1, state), lambda b, g: (b, g, 0, 0))],
        out_shape=[jax.ShapeDtypeStruct((batch, groups, n_chunks, width), F32),
                   jax.ShapeDtypeStruct((batch, groups, 1, state), F32)],
        compiler_params=_params("arbitrary", "arbitrary"),
        name="ssm_prompt",
    )(u_r, tm, sm, cm, dsk, coef)


def _ssm_sample_kernel(u_ref, h0_ref, h0s_ref, tm_ref, sm_ref, cm_ref, d_ref, coef_ref,
                       y_ref, h_ref):
    u = u_ref[...]
    h0 = h0_ref[...]
    coef = coef_ref[...]
    x = jnp.dot(u, sm_ref[...], precision=_HI, preferred_element_type=F32)
    h_ref[...] = coef[0:1] * h0 + coef[1:2] * h0s_ref[...] + x
    y = (jnp.dot(u, tm_ref[...], precision=_HI, preferred_element_type=F32)
         + jnp.dot(h0, cm_ref[...], precision=_HI, preferred_element_type=F32)
         + u * d_ref[...])
    y_ref[...] = _gelu(y)


def _ssm_sample(u_r, h0, h0s, tm, sm, cm, dsk, coef):
    groups, seqs, width = u_r.shape
    state = h0.shape[2]
    return pl.pallas_call(
        _ssm_sample_kernel,
        grid=(groups,),
        in_specs=[pl.BlockSpec((None, seqs, width), lambda g: (g, 0, 0)),
                  pl.BlockSpec((None, seqs, state), lambda g: (g, 0, 0)),
                  pl.BlockSpec((None, seqs, state), lambda g: (g, 0, 0)),
                  pl.BlockSpec((None, width, width), lambda g: (g, 0, 0)),
                  pl.BlockSpec((None, width, state), lambda g: (g, 0, 0)),
                  pl.BlockSpec((None, state, width), lambda g: (g, 0, 0)),
                  pl.BlockSpec((None, 1, width), lambda g: (g, 0, 0)),
                  pl.BlockSpec((None, 2, state), lambda g: (g, 0, 0))],
        out_specs=[pl.BlockSpec((None, seqs, width), lambda g: (g, 0, 0)),
                   pl.BlockSpec((None, seqs, state), lambda g: (g, 0, 0))],
        out_shape=[jax.ShapeDtypeStruct((groups, seqs, width), F32),
                   jax.ShapeDtypeStruct((groups, seqs, state), F32)],
        compiler_params=_params("arbitrary"),
        name="ssm_sample",
    )(u_r, h0, h0s, tm, sm, cm, dsk, coef)


def _ssm_tables(lam_re, lam_im, log_dt, b_re, b_im, c_re, c_im, d_skip, chunk, n_steps):
    groups, p_dim, ch = b_re.shape
    lr, li = lam_re.astype(F32), lam_im.astype(F32)
    dt = jnp.exp(log_dt.astype(F32))[:, None]
    mag = jnp.exp(lr * dt)
    a_re, a_im = mag * jnp.cos(li * dt), mag * jnp.sin(li * dt)
    den = lr * lr + li * li
    n_re, n_im = a_re - 1.0, a_im
    g_re = (n_re * lr + n_im * li) / den
    g_im = (n_im * lr - n_re * li) / den
    br, bi = b_re.astype(F32), b_im.astype(F32)
    bb_re = g_re[..., None] * br - g_im[..., None] * bi
    bb_im = g_re[..., None] * bi + g_im[..., None] * br

    def power(n):
        n = n.astype(F32)[:, None, None]
        m = jnp.exp(lr * dt * n)
        return m * jnp.cos(li * dt * n), m * jnp.sin(li * dt * n)

    taus = jnp.arange(chunk + 1)
    pw_re, pw_im = power(taus)
    ab_re = pw_re[:chunk, :, :, None] * bb_re - pw_im[:chunk, :, :, None] * bb_im
    ab_im = pw_re[:chunk, :, :, None] * bb_im + pw_im[:chunk, :, :, None] * bb_re
    cr, ci = c_re.astype(F32), c_im.astype(F32)
    kern = (jnp.einsum('gcp,tgpd->gtcd', cr, ab_re, precision=_HI)
            - jnp.einsum('gcp,tgpd->gtcd', ci, ab_im, precision=_HI))
    s_idx = jnp.arange(chunk)[:, None]
    t_idx = jnp.arange(chunk)[None, :]
    lag = t_idx - s_idx
    tm = jnp.where((lag >= 0)[None, :, :, None, None], kern[:, jnp.maximum(lag, 0)], 0.0)
    tm = tm.transpose(0, 1, 4, 2, 3).reshape(groups, chunk * ch, chunk * ch)
    sm_re = ab_re[::-1].transpose(1, 0, 3, 2).reshape(groups, chunk * ch, p_dim)
    sm_im = ab_im[::-1].transpose(1, 0, 3, 2).reshape(groups, chunk * ch, p_dim)
    sm = jnp.concatenate([sm_re, sm_im, sm_im, sm_re], axis=-1)
    ca_re = cr[:, None] * pw_re[1:].transpose(1, 0, 2)[:, :, None, :] \
        - ci[:, None] * pw_im[1:].transpose(1, 0, 2)[:, :, None, :]
    ca_im = cr[:, None] * pw_im[1:].transpose(1, 0, 2)[:, :, None, :] \
        + ci[:, None] * pw_re[1:].transpose(1, 0, 2)[:, :, None, :]
    cm = jnp.concatenate([ca_re, -ca_im], axis=-1)
    cm = cm.transpose(0, 3, 1, 2).reshape(groups, 2 * p_dim, chunk * ch)
    dsk = jnp.tile(d_skip.astype(F32), (1, chunk)).reshape(groups, 1, chunk * ch)
    steps = chunk * (2 ** jnp.arange(n_steps))
    sp_re, sp_im = power(steps)
    p_rows = jnp.concatenate([sp_re, sp_re], axis=-1)
    q_rows = jnp.concatenate([-sp_im, sp_im], axis=-1)
    coef = jnp.stack([p_rows, q_rows], axis=1).reshape(2 * n_steps, groups, 2 * p_dim)
    return tm, sm, cm, dsk, coef.transpose(1, 0, 2)


def _layer_norm(y, g, b, axis):
    mu = jnp.mean(y, axis=axis, keepdims=True)
    var = jnp.mean(jnp.square(y - mu), axis=axis, keepdims=True)
    return (y - mu) * lax.rsqrt(var + EPS) * g + b


def _out_kernel(x_ref, a_ref, yg_ref, wglu_ref, bglu_ref, wo_ref, g_ref, b_ref, x1_ref,
                *, d_attn, alpha):
    yg = yg_ref[...]
    gate = jax.nn.sigmoid(
        jnp.dot(yg.astype(BF16), wglu_ref[...], preferred_element_type=F32) + bglu_ref[...])
    s = (yg * gate).astype(BF16)
    h = (jnp.dot(a_ref[...], wo_ref[:d_attn, :], preferred_element_type=F32)
         + jnp.dot(s, wo_ref[d_attn:, :], preferred_element_type=F32))
    x1_ref[...] = _layer_norm(alpha * x_ref[...] + h, g_ref[...], b_ref[...], -1)


def _out_proj(x_all, a_all, yg_all, wglu_bf, bglu, wo_bf, ln_g, ln_b, alpha):
    tp, d_model = x_all.shape
    d_attn = a_all.shape[1]
    d_ssm = yg_all.shape[1]
    tm = PROJ_TM
    row = lambda i: (i, 0)
    fixed = lambda i: (0, 0)
    return pl.pallas_call(
        functools.partial(_out_kernel, d_attn=d_attn, alpha=alpha),
        grid=(tp // tm,),
        in_specs=[pl.BlockSpec((tm, d_model), row), pl.BlockSpec((tm, d_attn), row),
                  pl.BlockSpec((tm, d_ssm), row), pl.BlockSpec((d_ssm, d_ssm), fixed),
                  pl.BlockSpec((1, d_ssm), fixed), pl.BlockSpec((d_model, d_model), fixed),
                  pl.BlockSpec((1, d_model), fixed), pl.BlockSpec((1, d_model), fixed)],
        out_specs=pl.BlockSpec((tm, d_model), row),
        out_shape=jax.ShapeDtypeStruct((tp, d_model), F32),
        compiler_params=_params("arbitrary"),
        name="out_proj",
    )(x_all, a_all, yg_all, wglu_bf, bglu, wo_bf, ln_g, ln_b)


def _top_k_rows(s, k):
    n = s.shape[0]
    iota = lax.broadcasted_iota(jnp.int32, s.shape, 0).astype(F32)
    rank = jnp.full(s.shape, float(k), F32)
    vals, idxs = [], []
    for r in range(k):
        mx = jnp.max(s, axis=0, keepdims=True)
        ix = jnp.min(jnp.where(s == mx, iota, float(n)), axis=0, keepdims=True)
        hit = iota == ix
        rank = jnp.where(hit, float(r), rank)
        s = jnp.where(hit, NEG_INF, s)
        vals.append(mx)
        idxs.append(ix)
    return vals, idxs, rank


def _staircase(k):
    return [(a, b) for a in range(k) for b in range(k) if (a + 1) * (b + 1) <= k]


def _peer_sel_kernel(x1_ref, wq_ref, qg_ref, keys_ref, n1_ref, e1_ref, r2_ref, e2_ref,
                     *, n_heads, d_key, topk):
    qp = jnp.dot(x1_ref[...].astype(BF16), wq_ref[...], preferred_element_type=F32)
    d_half = d_key // 2
    pairs = _staircase(topk)
    n_pairs = len(pairs)
    n_rows = -(-n_pairs // 8) * 8
    tb = qp.shape[0]
    for h in range(n_heads):
        qh = qp[:, h * d_key:(h + 1) * d_key]
        qn = qh * lax.rsqrt(jnp.mean(qh * qh, axis=-1, keepdims=True) + EPS) * qg_ref[...]
        qb = qn.astype(BF16)
        s1 = lax.dot_general(keys_ref[2 * h], qb[:, :d_half], _NT, preferred_element_type=F32)
        s2 = lax.dot_general(keys_ref[2 * h + 1], qb[:, d_half:], _NT, preferred_element_type=F32)
        t1, i1, _ = _top_k_rows(s1, topk)
        t2, _, r2 = _top_k_rows(s2, topk)
        cand = jnp.concatenate(
            [t1[a] + t2[b] for a, b in pairs]
            + [jnp.full((n_rows - n_pairs, tb), NEG_INF, F32)], axis=0)
        iota = lax.broadcasted_iota(jnp.int32, cand.shape, 0).astype(F32)
        work = cand
        z = jnp.zeros((1, tb), F32)
        m = t1[0] + t2[0]
        for _ in range(topk):
            mx = jnp.max(work, axis=0, keepdims=True)
            ix = jnp.min(jnp.where(work == mx, iota, float(n_rows)), axis=0, keepdims=True)
            work = jnp.where(iota == ix, NEG_INF, work)
            z = z + jnp.exp(mx - m)
        chosen = jnp.where((work == NEG_INF) & (iota < float(n_pairs)), 1.0, 0.0)
        key_iota = lax.broadcasted_iota(jnp.int32, s1.shape, 0).astype(F32)
        n1 = jnp.zeros(s1.shape, F32)
        off = 0
        for a in range(topk):
            cnt = sum(1 for pa, _ in pairs if pa == a)
            n_a = jnp.sum(chosen[off:off + cnt], axis=0, keepdims=True)
            off += cnt
            n1 = jnp.where(key_iota == i1[a], n_a, n1)
        n1_ref[h] = n1.astype(n1_ref.dtype)
        e1_ref[h] = jnp.exp(s1 - t1[0]).astype(e1_ref.dtype)
        r2_ref[h] = pltpu.bitcast(r2.astype(BF16), r2_ref.dtype)
        e2_ref[h] = pltpu.bitcast((jnp.exp(s2 - t2[0]) / z).astype(BF16), e2_ref.dtype)


def _peer_select(x1, wq_bf, q_g, keys_bf, n_heads, d_key, n_keys):
    tp, d_model = x1.shape
    tb = SEL_TB
    row_spec = pl.BlockSpec((n_heads, n_keys, tb), lambda i: (0, 0, i))
    tile_spec = pl.BlockSpec((n_heads, n_keys // 2, tb), lambda i: (0, 0, i))
    out_shape = [jax.ShapeDtypeStruct((n_heads, n_keys, tp), F32)] * 2 + [
        jax.ShapeDtypeStruct((n_heads, n_keys // 2, tp), jnp.uint32)] * 2
    return pl.pallas_call(
        functools.partial(_peer_sel_kernel, n_heads=n_heads, d_key=d_key, topk=PEER_TOPK),
        grid=(tp // tb,),
        in_specs=[pl.BlockSpec((tb, d_model), lambda i: (i, 0)),
                  pl.BlockSpec(wq_bf.shape, lambda i: (0, 0)),
                  pl.BlockSpec((1, d_key), lambda i: (0, 0)),
                  pl.BlockSpec(keys_bf.shape, lambda i: (0, 0, 0))],
        out_specs=[row_spec, row_spec, tile_spec, tile_spec],
        out_shape=out_shape,
        compiler_params=_params("arbitrary"),
        name="peer_select",
    )(x1, wq_bf, q_g, keys_bf)


def _peer_dense_kernel(xt_ref, xtb_ref, u_ref, vt_ref, n1_ref, e1_ref, r2_ref, e2_ref,
                       g_ref, b_ref, o_ref, acc_ref, act_ref, wg_ref,
                       *, n_heads, n_keys, lane_tile, alpha):
    c = pl.program_id(1)
    n_chunks = pl.num_programs(1) - 1
    n_i = act_ref.shape[0] // n_keys
    tb = act_ref.shape[1]
    zero = jnp.zeros((n_keys, lane_tile), BF16)

    @pl.when(c == 0)
    def _():
        acc_ref[...] = jnp.zeros(acc_ref.shape, F32)
        wg_ref[1] = jnp.zeros(wg_ref.shape[1:], BF16)

    def row_tile(ref, h, il, cols):
        row = jnp.broadcast_to(ref[h, il:il + 1, cols], (BF16_SUBLANES, lane_tile)).astype(BF16)
        return jnp.tile(row, (n_keys // BF16_SUBLANES, 1))

    def pv_previous():
        acc_ref[...] += jnp.dot(vt_ref[...], wg_ref[(c + 1) % 2], preferred_element_type=F32)

    @pl.when(c < n_chunks)
    def _():
        pv_previous()
        act_ref[...] = jnp.dot(u_ref[...], xtb_ref[...], preferred_element_type=F32)
        cur = c % 2
        for tc in range(tb // lane_tile):
            cols = slice(tc * lane_tile, (tc + 1) * lane_tile)
            for il in range(n_i):
                w = zero
                for h in range(n_heads):
                    sel = pltpu.bitcast(r2_ref[h, :, cols], BF16) < row_tile(n1_ref, h, il, cols)
                    e2 = pltpu.bitcast(e2_ref[h, :, cols], BF16)
                    w = w + jnp.where(sel, e2, zero) * row_tile(e1_ref, h, il, cols)
                rows = slice(il * n_keys, (il + 1) * n_keys)
                wg_ref[cur, rows, cols] = w * _gelu(act_ref[rows, cols]).astype(BF16)

    @pl.when(c == n_chunks)
    def _():
        pv_previous()
        o_ref[...] = _layer_norm(alpha * xt_ref[...] + acc_ref[...], g_ref[...], b_ref[...], 0)


def _peer_dense(x1t, x1t_bf, u_bf, vt_bf, n1, e1, r2, e2, ln_g, ln_b, alpha):
    d_model, tp = x1t.shape
    n_heads, n_keys, _ = n1.shape
    n_exp = u_bf.shape[0]
    tb, ec = PEER_TB, PEER_EC
    n_i = ec // n_keys
    n_chunks = n_exp // ec
    tok = lambda t, c: (0, t)
    cur = lambda c: jnp.minimum(c, n_chunks - 1)
    prev = lambda c: jnp.maximum(c - 1, 0)
    return pl.pallas_call(
        functools.partial(_peer_dense_kernel, n_heads=n_heads, n_keys=n_keys,
                          lane_tile=PEER_LANE_TILE, alpha=alpha),
        grid=(tp // tb, n_chunks + 1),
        in_specs=[pl.BlockSpec((d_model, tb), tok),
                  pl.BlockSpec((d_model, tb), tok),
                  pl.BlockSpec((ec, d_model), lambda t, c: (cur(c), 0)),
                  pl.BlockSpec((d_model, ec), lambda t, c: (0, prev(c))),
                  pl.BlockSpec((n_heads, n_i, tb), lambda t, c: (0, cur(c), t)),
                  pl.BlockSpec((n_heads, n_i, tb), lambda t, c: (0, cur(c), t)),
                  pl.BlockSpec((n_heads, n_keys // 2, tb), lambda t, c: (0, 0, t)),
                  pl.BlockSpec((n_heads, n_keys // 2, tb), lambda t, c: (0, 0, t)),
                  pl.BlockSpec((d_model, 1), lambda t, c: (0, 0)),
                  pl.BlockSpec((d_model, 1), lambda t, c: (0, 0))],
        out_specs=pl.BlockSpec((d_model, tb), tok),
        out_shape=jax.ShapeDtypeStruct((d_model, tp), F32),
        scratch_shapes=[pltpu.VMEM((d_model, tb), F32), pltpu.VMEM((ec, tb), F32),
                        pltpu.VMEM((2, ec, tb), BF16)],
        compiler_params=_params("arbitrary", "arbitrary"),
        name="peer_dense",
    )(x1t, x1t_bf, u_bf, vt_bf, n1, e1, r2, e2, ln_g, ln_b)


def kernel(x_prompt, x_sample, cache_k, cache_v, state_ssm_re, state_ssm_im, page_table,
           w_in, lambda_q1, lambda_k1, lambda_q2, lambda_k2, attn_subln_g,
           ssm_lambda_re, ssm_lambda_im, ssm_log_dt, ssm_b_re, ssm_b_im, ssm_c_re, ssm_c_im,
           ssm_d, ssm_w_glu, ssm_b_glu, w_out, ln1_g, ln1_b,
           peer_w_q, peer_q_g, peer_sub_keys, peer_u, peer_v, ln2_g, ln2_b):
    depth = w_in.shape[0]
    batch, seq, d_model = x_prompt.shape
    dec_batch, dec_seq, _ = x_sample.shape
    _, n_phys, page, n_heads, _, dqk = cache_k.shape
    dv = cache_v.shape[-1]
    d_attn = n_heads * dv
    groups, p_dim = state_ssm_re.shape[2:]
    ch = ssm_b_re.shape[-1]
    peer_heads, _, n_keys, d_half = peer_sub_keys.shape[1:]
    d_key = 2 * d_half
    alpha = (2.0 * depth) ** 0.25
    scale = dqk ** -0.5

    n_prompt = batch * seq
    n_sample = dec_batch * dec_seq
    n_tok = n_prompt + n_sample
    tp = -(-n_tok // TOKEN_PAD) * TOKEN_PAD
    n_chunks = seq // SSM_CHUNK
    n_steps = int(math.log2(n_chunks))
    assert n_chunks == 1 << n_steps and seq % ATTN_TQ == 0 and n_prompt % TOKEN_PAD == 0

    x_all = jnp.concatenate(
        [x_prompt.reshape(n_prompt, d_model), x_sample.reshape(n_sample, d_model),
         jnp.zeros((tp - n_tok, d_model), x_prompt.dtype)], axis=0)
    slope_vals = 2.0 ** (-8.0 * jnp.arange(1, n_heads + 1, dtype=F32) / n_heads)
    slopes = jnp.broadcast_to(slope_vals[:, None, None], (n_heads, 1, 128))
    slopecol = jnp.repeat(slope_vals, 2 * dec_seq)[:, None]
    eye_hc = jnp.eye(2 * n_heads, dtype=F32)

    outs = {name: [] for name in ("kp", "vp", "hpr", "hpi", "ks", "vs", "hsr", "hsi")}
    for l in range(depth):
        lam_init = 0.8 - 0.6 * math.exp(-0.3 * l)
        lamv = jnp.stack([lambda_q1[l], lambda_k1[l], lambda_q2[l], lambda_k2[l]]).astype(F32)
        subln_g = attn_subln_g[l].astype(F32)[None, :]

        w_bf = w_in[l].astype(BF16)
        kt_prompt, k_last, v_all, u_all, qkv_bf = _project(
            x_all, w_bf, w_bf[:, d_attn:2 * d_attn].T, d_attn, scale, batch, seq)

        a_prompt = _attn_prompt(qkv_bf, lamv, subln_g, slopes, batch, seq, n_heads, dv, dqk,
                                lam_init)

        qkv_s = qkv_bf[n_prompt:n_tok].reshape(dec_batch, dec_seq, 3 * d_attn)
        q_s = qkv_s[..., :d_attn].reshape(dec_batch, dec_seq, 2 * n_heads, dqk)
        qbd = (q_s.transpose(0, 2, 1, 3)[:, :, :, None, :].astype(F32)
               * eye_hc[None, :, None, :, None]).astype(BF16)
        qbd = qbd.reshape(dec_batch, 2 * n_heads * dec_seq, d_attn)
        k_new = jnp.pad(qkv_s[..., d_attn:2 * d_attn].transpose(0, 2, 1),
                        ((0, 0), (0, 0), (0, page - dec_seq)))
        v_new = jnp.pad(qkv_s[..., 2 * d_attn:], ((0, 0), (0, page - dec_seq), (0, 0)))
        kt2d = jnp.transpose(cache_k[l], (0, 2, 3, 4, 1)).reshape(n_phys * d_attn, page)
        a_s = _attn_sample(page_table, lamv, subln_g, slopecol, qbd, kt2d,
                           cache_v[l].reshape(n_phys * page * n_heads, dv),
                           k_new, v_new, n_heads, dec_seq, lam_init)
        a_sample = a_s.reshape(dec_batch, n_heads, dec_seq, dv).transpose(0, 2, 1, 3)
        a_all = jnp.concatenate(
            [a_prompt, a_sample.reshape(n_sample, d_attn),
             jnp.zeros((tp - n_tok, d_attn), BF16)], axis=0)

        ssm_p = (ssm_lambda_re[l], ssm_lambda_im[l], ssm_log_dt[l], ssm_b_re[l], ssm_b_im[l],
                 ssm_c_re[l], ssm_c_im[l], ssm_d[l])
        tm, sm, cm, dsk, coef = _ssm_tables(*ssm_p, SSM_CHUNK, n_steps)
        u_r = (u_all[:n_prompt].reshape(batch, n_chunks, SSM_CHUNK, groups, ch)
               .transpose(0, 3, 1, 2, 4).reshape(batch, groups, n_chunks, SSM_CHUNK * ch))
        y_r, h_p = _ssm_prompt(u_r, tm, sm, cm, dsk, coef)
        yg_prompt = (y_r.reshape(batch, groups, n_chunks, SSM_CHUNK, ch)
                     .transpose(0, 2, 3, 1, 4).reshape(n_prompt, groups * ch))

        tm_s, sm_s, cm_s, dsk_s, coef_s = _ssm_tables(*ssm_p, dec_seq, 1)
        us_r = (u_all[n_prompt:n_tok].reshape(dec_batch, dec_seq, groups, ch)
                .transpose(2, 0, 1, 3).reshape(groups, dec_batch, dec_seq * ch))
        h0r = state_ssm_re[l].astype(F32).transpose(1, 0, 2)
        h0i = state_ssm_im[l].astype(F32).transpose(1, 0, 2)
        ys_r, h_s = _ssm_sample(us_r, jnp.concatenate([h0r, h0i], -1),
                                jnp.concatenate([h0i, h0r], -1),
                                tm_s, sm_s[..., :2 * p_dim], cm_s, dsk_s, coef_s)
        yg_sample = (ys_r.reshape(groups, dec_batch, dec_seq, ch)
                     .transpose(1, 2, 0, 3).reshape(n_sample, groups * ch))
        yg_all = jnp.concatenate(
            [yg_prompt, yg_sample, jnp.zeros((tp - n_tok, groups * ch), F32)], axis=0)

        x1 = _out_proj(x_all, a_all, yg_all, ssm_w_glu[l].astype(BF16),
                       ssm_b_glu[l].astype(F32)[None, :], w_out[l].astype(BF16),
                       ln1_g[l].astype(F32)[None, :], ln1_b[l].astype(F32)[None, :], alpha)

        keys_bf = peer_sub_keys[l].reshape(2 * peer_heads, n_keys, d_half).astype(BF16)
        n1, e1, r2, e2 = _peer_select(x1, peer_w_q[l].astype(BF16),
                                      peer_q_g[l].astype(F32)[None, :], keys_bf,
                                      peer_heads, d_key, n_keys)
        x1t = x1.T
        out_t = _peer_dense(x1t, x1t.astype(BF16), peer_u[l].astype(BF16),
                            peer_v[l].T.astype(BF16), n1, e1, r2, e2,
                            ln2_g[l].astype(F32)[:, None], ln2_b[l].astype(F32)[:, None], alpha)
        x_all = out_t.T

        outs["kp"].append(kt_prompt.reshape(batch, n_heads, 2, dqk, seq).transpose(0, 4, 1, 2, 3))
        outs["vp"].append(v_all[:n_prompt].reshape(batch, seq, n_heads, dv))
        outs["hpr"].append(h_p[:, :, 0, :p_dim])
        outs["hpi"].append(h_p[:, :, 0, p_dim:])
        outs["ks"].append(k_last[:n_sample].reshape(dec_batch, dec_seq, n_heads, 2, dqk))
        outs["vs"].append(v_all[n_prompt:n_tok].reshape(dec_batch, dec_seq, n_heads, dv))
        outs["hsr"].append(h_s[:, :, :p_dim].transpose(1, 0, 2))
        outs["hsi"].append(h_s[:, :, p_dim:].transpose(1, 0, 2))

    y_prompt = x_all[:n_prompt].reshape(batch, seq, d_model)
    y_sample = x_all[n_prompt:n_tok].reshape(dec_batch, dec_seq, d_model)
    st = lambda name: jnp.stack(outs[name])
    return (y_prompt, y_sample, st("kp"), st("vp"), st("hpr"), st("hpi"),
            st("ks"), st("vs"), st("hsr"), st("hsi"))
```

```python
import functools
import math

import jax
import jax.numpy as jnp
from jax import lax
from jax.experimental import pallas as pl
from jax.experimental.pallas import tpu as pltpu

F32 = jnp.float32
BF16 = jnp.bfloat16

EPS = 1e-5
MASK_VALUE = -1e30
PEER_TOPK = 16
NEG_INF = float("-inf")

TOKEN_PAD = 512
PROJ_TM = 512
ATTN_TQ = 1024
ATTN_TK = 1024
ATTN_RS = 512
SAMPLE_PAGES = 16
SSM_CHUNK = 16
SEL_TB = 256
PEER_TB = 512
PEER_EC = 1024
BF16_SUBLANES = 16
PEER_LANE_TILE = 128
VMEM_LIMIT = 56 * 1024 * 1024

_NT = (((1,), (1,)), ((), ()))
_HI = lax.Precision.HIGHEST


def _params(*sem):
    return pltpu.CompilerParams(dimension_semantics=sem, vmem_limit_bytes=VMEM_LIMIT)


def _gelu(x):
    return 0.5 * x * (1.0 + lax.erf(x * (2.0 ** -0.5)))


def _proj_kernel(x_ref, w_ref, wkt_ref, kt_ref, ks_ref, v_ref, u_ref, qkv_ref,
                 *, d_attn, scale, n_prompt_blocks):
    i = pl.program_id(0)
    x = x_ref[...].astype(BF16)
    z = jnp.dot(x, w_ref[...], preferred_element_type=F32)
    k = z[:, d_attn:2 * d_attn]
    v = z[:, 2 * d_attn:3 * d_attn]
    v_ref[...] = v
    u_ref[...] = z[:, 3 * d_attn:]
    qkv_ref[:, :d_attn] = (z[:, :d_attn] * scale).astype(BF16)
    qkv_ref[:, d_attn:2 * d_attn] = k.astype(BF16)
    qkv_ref[:, 2 * d_attn:] = v.astype(BF16)

    @pl.when(i < n_prompt_blocks)
    def _():
        kt_ref[...] = lax.dot_general(wkt_ref[...], x, _NT, preferred_element_type=F32)

    @pl.when(i == pl.num_programs(0) - 1)
    def _():
        ks_ref[...] = k


def _project(x_all, w_bf, wkt_bf, d_attn, scale, batch, seq):
    tp, d_model = x_all.shape
    n_out = w_bf.shape[1]
    d_u = n_out - 3 * d_attn
    tm = PROJ_TM
    per_seq = seq // tm
    n_prompt_blocks = batch * per_seq
    assert tp // tm == n_prompt_blocks + 1
    last = n_prompt_blocks - 1
    row = lambda i: (i, 0)
    kt_map = lambda i: (jnp.minimum(i, last) // per_seq, 0, jnp.minimum(i, last) % per_seq)
    return pl.pallas_call(
        functools.partial(_proj_kernel, d_attn=d_attn, scale=scale,
                          n_prompt_blocks=n_prompt_blocks),
        grid=(tp // tm,),
        in_specs=[pl.BlockSpec((tm, d_model), row),
                  pl.BlockSpec((d_model, n_out), lambda i: (0, 0)),
                  pl.BlockSpec((d_attn, d_model), lambda i: (0, 0))],
        out_specs=[pl.BlockSpec((None, d_attn, tm), kt_map),
                   pl.BlockSpec((tm, d_attn), lambda i: (0, 0)),
                   pl.BlockSpec((tm, d_attn), row),
                   pl.BlockSpec((tm, d_u), row),
                   pl.BlockSpec((tm, 3 * d_attn), row)],
        out_shape=[jax.ShapeDtypeStruct((batch, d_attn, seq), F32),
                   jax.ShapeDtypeStruct((tm, d_attn), F32),
                   jax.ShapeDtypeStruct((tp, d_attn), F32),
                   jax.ShapeDtypeStruct((tp, d_u), F32),
                   jax.ShapeDtypeStruct((tp, 3 * d_attn), BF16)],
        compiler_params=_params("arbitrary"),
        name="proj",
    )(x_all, w_bf, wkt_bf)


def _lam_value(lamv_ref, lam_init):
    lv = lamv_ref[...]
    e1 = jnp.exp(jnp.sum(lv[0:1] * lv[1:2], axis=1, keepdims=True))
    e2 = jnp.exp(jnp.sum(lv[2:3] * lv[3:4], axis=1, keepdims=True))
    return e1 - e2 + lam_init


def _subln(o, g, lam_init):
    ms = jnp.mean(o * o, axis=-1, keepdims=True)
    return o * lax.rsqrt(ms + EPS) * g * (1.0 - lam_init)


def _attn_prompt_kernel(qi_ref, kj_ref, lamv_ref, g_ref, slope_ref, q_ref, k_ref, v_ref, o_ref,
                        m_ref, l_ref, acc_ref, *, tq, tk, rs, dqk, lam_init):
    t = pl.program_id(2)
    i = qi_ref[t]
    j = kj_ref[t]
    lanes = m_ref.shape[-1]
    ratio = tq // tk
    off = j - ratio * i

    @pl.when(j == 0)
    def _():
        m_ref[...] = jnp.full(m_ref.shape, MASK_VALUE, F32)
        l_ref[...] = jnp.zeros(l_ref.shape, F32)
        acc_ref[...] = jnp.zeros(acc_ref.shape, F32)

    n_sub = tq // rs

    def step(shift):
        kcol = lax.broadcasted_iota(jnp.int32, (1, tk), 1) + (j * tk - i * tq)
        colbias = slope_ref[...][:, :1] * kcol.astype(F32)

        def n_keys(r):
            return tk if shift is None else max(0, min(tk, (r + 1) * rs - shift))

        def masked(r):
            return shift is not None and 0 < (r + 1) * rs - shift <= tk

        def scores(r):
            q = q_ref[r * rs:(r + 1) * rs, :]
            lane = lax.broadcasted_iota(jnp.int32, q.shape, 1)
            zero = jnp.zeros_like(q)
            q2 = jnp.concatenate([jnp.where(lane < dqk, q, zero), jnp.where(lane < dqk, zero, q)], axis=0)
            return lax.dot_general(q2, k_ref[:n_keys(r), :], _NT, preferred_element_type=F32)

        blocks = [r for r in range(n_sub) if n_keys(r) > 0]
        ahead = 2
        pending = {r: scores(r) for r in blocks[:ahead]}
        for n, r in enumerate(blocks):
            if n + ahead < len(blocks):
                pending[blocks[n + ahead]] = scores(blocks[n + ahead])
            nk = n_keys(r)
            rows = slice(2 * r * rs, 2 * (r + 1) * rs)
            s = pending.pop(r) + colbias[:, :nk]
            if masked(r):
                row = lax.broadcasted_iota(jnp.int32, (2 * rs, nk), 0) & (rs - 1)
                col = lax.broadcasted_iota(jnp.int32, (2 * rs, nk), 1)
                s = jnp.where(row + (r * rs - shift) >= col, s, MASK_VALUE)
            m_prev = m_ref[rows, :]
            m_new = jnp.maximum(m_prev, jnp.max(s, axis=1, keepdims=True))
            alpha = jnp.exp(m_prev - m_new)
            p = jnp.exp(s - jnp.tile(m_new, (1, nk // lanes)))
            l_ref[rows, :] = alpha * l_ref[rows, :] + jnp.sum(p, axis=1, keepdims=True)
            acc_ref[rows, :] = alpha * acc_ref[rows, :] + jnp.dot(
                p.astype(BF16), v_ref[:nk, :], preferred_element_type=F32)
            m_ref[rows, :] = m_new

    @pl.when(off < 0)
    def _():
        step(None)

    for d in range(ratio):
        pl.when(off == d)(functools.partial(step, d * tk))

    @pl.when(off == ratio - 1)
    def _():
        lam = _lam_value(lamv_ref, lam_init)
        for r in range(n_sub):
            r1 = slice(2 * r * rs, (2 * r + 1) * rs)
            r2 = slice((2 * r + 1) * rs, 2 * (r + 1) * rs)
            o = acc_ref[r1, :] / l_ref[r1, :] - lam * (acc_ref[r2, :] / l_ref[r2, :])
            o_ref[r * rs:(r + 1) * rs, :] = _subln(o, g_ref[...], lam_init).astype(o_ref.dtype)


def _attn_prompt(qkv_bf, lamv, subln_g, slopes, batch, seq, n_heads, dv, dqk, lam_init):
    tq, tk = ATTN_TQ, ATTN_TK
    nq, nk = seq // tq, seq // tk
    ratio = tq // tk
    d_attn = n_heads * dv
    pairs = [(i, j) for i in range(nq) for j in range(ratio * (i + 1))]
    qi = jnp.asarray([p[0] for p in pairs], jnp.int32)
    kj = jnp.asarray([p[1] for p in pairs], jnp.int32)
    grid_spec = pltpu.PrefetchScalarGridSpec(
        num_scalar_prefetch=2,
        grid=(batch, n_heads, len(pairs)),
        in_specs=[pl.BlockSpec(lamv.shape, lambda b, h, t, qi, kj: (0, 0)),
                  pl.BlockSpec((1, dv), lambda b, h, t, qi, kj: (0, 0)),
                  pl.BlockSpec((None, 1, 128), lambda b, h, t, qi, kj: (h, 0, 0)),
                  pl.BlockSpec((tq, dv), lambda b, h, t, qi, kj: (b * nq + qi[t], h)),
                  pl.BlockSpec((tk, dv), lambda b, h, t, qi, kj: (b * nk + kj[t], n_heads + h)),
                  pl.BlockSpec((tk, dv), lambda b, h, t, qi, kj: (b * nk + kj[t], 2 * n_heads + h))],
        out_specs=pl.BlockSpec((tq, dv), lambda b, h, t, qi, kj: (b * nq + qi[t], h)),
        scratch_shapes=[pltpu.VMEM((2 * tq, dv), F32), pltpu.VMEM((2 * tq, dv), F32),
                        pltpu.VMEM((2 * tq, dv), F32)],
    )
    return pl.pallas_call(
        functools.partial(_attn_prompt_kernel, tq=tq, tk=tk, rs=ATTN_RS, dqk=dqk, lam_init=lam_init),
        grid_spec=grid_spec,
        out_shape=jax.ShapeDtypeStruct((batch * seq, d_attn), BF16),
        compiler_params=_params("arbitrary", "arbitrary", "arbitrary"),
        name="attn_prompt",
    )(qi, kj, lamv, subln_g, slopes, qkv_bf, qkv_bf, qkv_bf)


def _attn_sample_kernel(pt_ref, lamv_ref, g_ref, slopecol_ref, qbd_ref, kn_ref, vn_ref, *rest,
                        pp, page, past_len, n_heads, n_new, lam_init):
    del pt_ref
    k_refs, v_refs = rest[:pp], rest[pp:2 * pp]
    o_ref, m_ref, l_ref, acc_ref = rest[2 * pp:]
    step = pl.program_id(1)
    rows_h = 2 * n_new
    dv = m_ref.shape[1]

    @pl.when(step == 0)
    def _():
        m_ref[...] = jnp.full(m_ref.shape, MASK_VALUE, F32)
        l_ref[...] = jnp.zeros(l_ref.shape, F32)
        acc_ref[...] = jnp.zeros(acc_ref.shape, F32)

    def update(s, v):
        m_prev = m_ref[...]
        m_new = jnp.maximum(m_prev, jnp.max(s, axis=1, keepdims=True))
        alpha = jnp.exp(m_prev - m_new)
        p = jnp.exp(s - m_new)
        l_ref[...] = alpha * l_ref[...] + jnp.sum(p, axis=1, keepdims=True)
        acc_ref[...] = jnp.tile(alpha, (1, n_heads)) * acc_ref[...] + jnp.dot(
            p.astype(BF16), v, preferred_element_type=F32)
        m_ref[...] = m_new

    def page_values(n):
        return jnp.concatenate(
            [v_refs[n][pl.ds(h, page, stride=n_heads), :] for h in range(n_heads)], axis=1).astype(BF16)

    slope = slopecol_ref[...]
    key = lax.broadcasted_iota(jnp.int32, (1, page), 1)

    def page_scores(n):
        return jnp.dot(qbd_ref[...], k_refs[n][...].astype(BF16), preferred_element_type=F32)

    ahead = 2
    pending = {n: page_scores(n) for n in range(min(ahead, pp))}
    for n in range(pp):
        if n + ahead < pp:
            pending[n + ahead] = page_scores(n + ahead)
        kpos = key + ((step * pp + n) * page - past_len)
        s = pending.pop(n) + slope * kpos.astype(F32)
        update(s, page_values(n))

    @pl.when(step == pl.num_programs(1) - 1)
    def _():
        s = jnp.dot(qbd_ref[...], kn_ref[...], preferred_element_type=F32)
        col = lax.broadcasted_iota(jnp.int32, s.shape, 1)
        qi = lax.broadcasted_iota(jnp.int32, s.shape, 0) & (n_new - 1)
        s = s + slope * col.astype(F32)
        update(jnp.where(col <= qi, s, MASK_VALUE), vn_ref[...])
        lam = _lam_value(lamv_ref, lam_init)
        for h in range(n_heads):
            r = slice(h * rows_h, (h + 1) * rows_h)
            a = acc_ref[r, h * dv:(h + 1) * dv] / l_ref[r, :]
            o = a[:n_new] - lam * a[n_new:]
            o_ref[h * n_new:(h + 1) * n_new, :] = _subln(o, g_ref[...], lam_init).astype(o_ref.dtype)


def _attn_sample(page_table, lamv, subln_g, slopecol, qbd, kt2d, v2d, k_new, v_new,
                 n_heads, n_new, lam_init):
    dec_batch, n_pages = page_table.shape
    page, dv = kt2d.shape[1], v2d.shape[1]
    n_rows, d_attn = qbd.shape[1:]
    pp = SAMPLE_PAGES
    assert n_pages % pp == 0
    pt_flat = page_table.reshape(-1)
    fixed = lambda b, s, pt: (0, 0)
    seq3 = lambda b, s, pt: (b, 0, 0)

    def page_map(n):
        return lambda b, s, pt: (pt[b * n_pages + s * pp + n], 0)

    kernel = functools.partial(
        _attn_sample_kernel, pp=pp, page=page, past_len=n_pages * page,
        n_heads=n_heads, n_new=n_new, lam_init=lam_init)
    grid_spec = pltpu.PrefetchScalarGridSpec(
        num_scalar_prefetch=1,
        grid=(dec_batch, n_pages // pp),
        in_specs=[pl.BlockSpec(lamv.shape, fixed),
                  pl.BlockSpec((1, dv), fixed),
                  pl.BlockSpec((n_rows, 1), fixed),
                  pl.BlockSpec((None, n_rows, d_attn), seq3),
                  pl.BlockSpec((None,) + k_new.shape[1:], seq3),
                  pl.BlockSpec((None,) + v_new.shape[1:], seq3)]
        + [pl.BlockSpec((d_attn, page), page_map(n)) for n in range(pp)]
        + [pl.BlockSpec((page * n_heads, dv), page_map(n)) for n in range(pp)],
        out_specs=pl.BlockSpec((None, n_heads * n_new, dv), seq3),
        scratch_shapes=[pltpu.VMEM((n_rows, dv), F32), pltpu.VMEM((n_rows, dv), F32),
                        pltpu.VMEM((n_rows, n_heads * dv), F32)],
    )
    return pl.pallas_call(
        kernel,
        grid_spec=grid_spec,
        out_shape=jax.ShapeDtypeStruct((dec_batch, n_heads * n_new, dv), BF16),
        compiler_params=_params("arbitrary", "arbitrary"),
        name="attn_sample",
    )(pt_flat, lamv, subln_g, slopecol, qbd, k_new, v_new, *([kt2d] * pp), *([v2d] * pp))


def _ssm_prompt_kernel(u_ref, tm_ref, sm_ref, cm_ref, d_ref, coef_ref, y_ref, h_ref,
                       *, n_chunks, n_steps):
    u = u_ref[...]
    x = jnp.dot(u, sm_ref[...], precision=_HI, preferred_element_type=F32)
    half = x.shape[1] // 2
    x0, x1 = x[:, :half], x[:, half:]
    row = lax.broadcasted_iota(jnp.int32, x0.shape, 0)

    def shifted(a, d):
        return jnp.where(row >= d, pltpu.roll(a, d, 0), 0.0)

    y0, y1 = shifted(x0, 1), shifted(x1, 1)
    coef = coef_ref[...]
    for s in range(n_steps):
        d = 1 << s
        p = coef[2 * s:2 * s + 1]
        q = coef[2 * s + 1:2 * s + 2]
        s0, s1 = shifted(y0, d), shifted(y1, d)
        y0, y1 = y0 + p * s0 + q * s1, y1 + p * s1 - q * s0
    last = n_chunks - 1
    h_ref[...] = (coef[0:1] * y0[last:last + 1] + coef[1:2] * y1[last:last + 1]
                  + x0[last:last + 1])
    y = (jnp.dot(u, tm_ref[...], precision=_HI, preferred_element_type=F32)
         + jnp.dot(y0, cm_ref[...], precision=_HI, preferred_element_type=F32)
         + u * d_ref[...])
    y_ref[...] = _gelu(y)


def _ssm_prompt(u_r, tm, sm, cm, dsk, coef):
    batch, groups, n_chunks, width = u_r.shape
    n_steps = coef.shape[1] // 2
    state = cm.shape[1]
    return pl.pallas_call(
        functools.partial(_ssm_prompt_kernel, n_chunks=n_chunks, n_steps=n_steps),
        grid=(batch, groups),
        in_specs=[pl.BlockSpec((None, None, n_chunks, width), lambda b, g: (b, g, 0, 0)),
                  pl.BlockSpec((None, width, width), lambda b, g: (g, 0, 0)),
                  pl.BlockSpec((None, width, 2 * state), lambda b, g: (g, 0, 0)),
                  pl.BlockSpec((None, state, width), lambda b, g: (g, 0, 0)),
                  pl.BlockSpec((None, 1, width), lambda b, g: (g, 0, 0)),
                  pl.BlockSpec((None, 2 * n_steps, state), lambda b, g: (g, 0, 0))],
        out_specs=[pl.BlockSpec((None, None, n_chunks, width), lambda b, g: (b, g, 0, 0)),
                   pl.BlockSpec((None, None, 1, state), lambda b, g: (b, g, 0, 0))],
        out_shape=[jax.ShapeDtypeStruct((batch, groups, n_chunks, width), F32),
                   jax.ShapeDtypeStruct((batch, groups, 1, state), F32)],
        compiler_params=_params("arbitrary", "arbitrary"),
        name="ssm_prompt",
    )(u_r, tm, sm, cm, dsk, coef)


def _ssm_sample_kernel(u_ref, h0_ref, h0s_ref, tm_ref, sm_ref, cm_ref, d_ref, coef_ref,
                       y_ref, h_ref):
    u = u_ref[...]
    h0 = h0_ref[...]
    coef = coef_ref[...]
    x = jnp.dot(u, sm_ref[...], precision=_HI, preferred_element_type=F32)
    h_ref[...] = coef[0:1] * h0 + coef[1:2] * h0s_ref[...] + x
    y = (jnp.dot(u, tm_ref[...], precision=_HI, preferred_element_type=F32)
         + jnp.dot(h0, cm_ref[...], precision=_HI, preferred_element_type=F32)
         + u * d_ref[...])
    y_ref[...] = _gelu(y)


def _ssm_sample(u_r, h0, h0s, tm, sm, cm, dsk, coef):
    groups, seqs, width = u_r.shape
    state = h0.shape[2]
    return pl.pallas_call(
        _ssm_sample_kernel,
        grid=(groups,),
        in_specs=[pl.BlockSpec((None, seqs, width), lambda g: (g, 0, 0)),
                  pl.BlockSpec((None, seqs, state), lambda g: (g, 0, 0)),
                  pl.BlockSpec((None, seqs, state), lambda g: (g, 0, 0)),
                  pl.BlockSpec((None, width, width), lambda g: (g, 0, 0)),
                  pl.BlockSpec((None, width, state), lambda g: (g, 0, 0)),
                  pl.BlockSpec((None, state, width), lambda g: (g, 0, 0)),
                  pl.BlockSpec((None, 1, width), lambda g: (g, 0, 0)),
                  pl.BlockSpec((None, 2, state), lambda g: (g, 0, 0))],
        out_specs=[pl.BlockSpec((None, seqs, width), lambda g: (g, 0, 0)),
                   pl.BlockSpec((None, seqs, state), lambda g: (g, 0, 0))],
        out_shape=[jax.ShapeDtypeStruct((groups, seqs, width), F32),
                   jax.ShapeDtypeStruct((groups, seqs, state), F32)],
        compiler_params=_params("arbitrary"),
        name="ssm_sample",
    )(u_r, h0, h0s, tm, sm, cm, dsk, coef)


def _ssm_tables(lam_re, lam_im, log_dt, b_re, b_im, c_re, c_im, d_skip, chunk, n_steps):
    groups, p_dim, ch = b_re.shape
    lr, li = lam_re.astype(F32), lam_im.astype(F32)
    dt = jnp.exp(log_dt.astype(F32))[:, None]
    mag = jnp.exp(lr * dt)
    a_re, a_im = mag * jnp.cos(li * dt), mag * jnp.sin(li * dt)
    den = lr * lr + li * li
    n_re, n_im = a_re - 1.0, a_im
    g_re = (n_re * lr + n_im * li) / den
    g_im = (n_im * lr - n_re * li) / den
    br, bi = b_re.astype(F32), b_im.astype(F32)
    bb_re = g_re[..., None] * br - g_im[..., None] * bi
    bb_im = g_re[..., None] * bi + g_im[..., None] * br

    def power(n):
        n = n.astype(F32)[:, None, None]
        m = jnp.exp(lr * dt * n)
        return m * jnp.cos(li * dt * n), m * jnp.sin(li * dt * n)

    taus = jnp.arange(chunk + 1)
    pw_re, pw_im = power(taus)
    ab_re = pw_re[:chunk, :, :, None] * bb_re - pw_im[:chunk, :, :, None] * bb_im
    ab_im = pw_re[:chunk, :, :, None] * bb_im + pw_im[:chunk, :, :, None] * bb_re
    cr, ci = c_re.astype(F32), c_im.astype(F32)
    kern = (jnp.einsum('gcp,tgpd->gtcd', cr, ab_re, precision=_HI)
            - jnp.einsum('gcp,tgpd->gtcd', ci, ab_im, precision=_HI))
    s_idx = jnp.arange(chunk)[:, None]
    t_idx = jnp.arange(chunk)[None, :]
    lag = t_idx - s_idx
    tm = jnp.where((lag >= 0)[None, :, :, None, None], kern[:, jnp.maximum(lag, 0)], 0.0)
    tm = tm.transpose(0, 1, 4, 2, 3).reshape(groups, chunk * ch, chunk * ch)
    sm_re = ab_re[::-1].transpose(1, 0, 3, 2).reshape(groups, chunk * ch, p_dim)
    sm_im = ab_im[::-1].transpose(1, 0, 3, 2).reshape(groups, chunk * ch, p_dim)
    sm = jnp.concatenate([sm_re, sm_im, sm_im, sm_re], axis=-1)
    ca_re = cr[:, None] * pw_re[1:].transpose(1, 0, 2)[:, :, None, :] \
        - ci[:, None] * pw_im[1:].transpose(1, 0, 2)[:, :, None, :]
    ca_im = cr[:, None] * pw_im[1:].transpose(1, 0, 2)[:, :, None, :] \
        + ci[:, None] * pw_re[1:].transpose(1, 0, 2)[:, :, None, :]
    cm = jnp.concatenate([ca_re, -ca_im], axis=-1)
    cm = cm.transpose(0, 3, 1, 2).reshape(groups, 2 * p_dim, chunk * ch)
    dsk = jnp.tile(d_skip.astype(F32), (1, chunk)).reshape(groups, 1, chunk * ch)
    steps = chunk * (2 ** jnp.arange(n_steps))
    sp_re, sp_im = power(steps)
    p_rows = jnp.concatenate([sp_re, sp_re], axis=-1)
    q_rows = jnp.concatenate([-sp_im, sp_im], axis=-1)
    coef = jnp.stack([p_rows, q_rows], axis=1).reshape(2 * n_steps, groups, 2 * p_dim)
    return tm, sm, cm, dsk, coef.transpose(1, 0, 2)


def _layer_norm(y, g, b, axis):
    mu = jnp.mean(y, axis=axis, keepdims=True)
    var = jnp.mean(jnp.square(y - mu), axis=axis, keepdims=True)
    return (y - mu) * lax.rsqrt(var + EPS) * g + b


def _out_kernel(x_ref, a_ref, yg_ref, wglu_ref, bglu_ref, wo_ref, g_ref, b_ref, x1_ref,
                *, d_attn, alpha):
    yg = yg_ref[...]
    gate = jax.nn.sigmoid(
        jnp.dot(yg.astype(BF16), wglu_ref[...], preferred_element_type=F32) + bglu_ref[...])
    s = (yg * gate).astype(BF16)
    h = (jnp.dot(a_ref[...], wo_ref[:d_attn, :], preferred_element_type=F32)
         + jnp.dot(s, wo_ref[d_attn:, :], preferred_element_type=F32))
    x1_ref[...] = _layer_norm(alpha * x_ref[...] + h, g_ref[...], b_ref[...], -1)


def _out_proj(x_all, a_all, yg_all, wglu_bf, bglu, wo_bf, ln_g, ln_b, alpha):
    tp, d_model = x_all.shape
    d_attn = a_all.shape[1]
    d_ssm = yg_all.shape[1]
    tm = PROJ_TM
    row = lambda i: (i, 0)
    fixed = lambda i: (0, 0)
    return pl.pallas_call(
        functools.partial(_out_kernel, d_attn=d_attn, alpha=alpha),
        grid=(tp // tm,),
        in_specs=[pl.BlockSpec((tm, d_model), row), pl.BlockSpec((tm, d_attn), row),
                  pl.BlockSpec((tm, d_ssm), row), pl.BlockSpec((d_ssm, d_ssm), fixed),
                  pl.BlockSpec((1, d_ssm), fixed), pl.BlockSpec((d_model, d_model), fixed),
                  pl.BlockSpec((1, d_model), fixed), pl.BlockSpec((1, d_model), fixed)],
        out_specs=pl.BlockSpec((tm, d_model), row),
        out_shape=jax.ShapeDtypeStruct((tp, d_model), F32),
        compiler_params=_params("arbitrary"),
        name="out_proj",
    )(x_all, a_all, yg_all, wglu_bf, bglu, wo_bf, ln_g, ln_b)


def _top_k_rows(s, k):
    n = s.shape[0]
    iota = lax.broadcasted_iota(jnp.int32, s.shape, 0).astype(F32)
    rank = jnp.full(s.shape, float(k), F32)
    vals, idxs = [], []
    for r in range(k):
        mx = jnp.max(s, axis=0, keepdims=True)
        ix = jnp.min(jnp.where(s == mx, iota, float(n)), axis=0, keepdims=True)
        hit = iota == ix
        rank = jnp.where(hit, float(r), rank)
        s = jnp.where(hit, NEG_INF, s)
        vals.append(mx)
        idxs.append(ix)
    return vals, idxs, rank


def _staircase(k):
    return [(a, b) for a in range(k) for b in range(k) if (a + 1) * (b + 1) <= k]


def _peer_sel_kernel(x1_ref, wq_ref, qg_ref, keys_ref, n1_ref, e1_ref, r2_ref, e2_ref,
                     *, n_heads, d_key, topk):
    qp = jnp.dot(x1_ref[...].astype(BF16), wq_ref[...], preferred_element_type=F32)
    d_half = d_key // 2
    pairs = _staircase(topk)
    n_pairs = len(pairs)
    n_rows = -(-n_pairs // 8) * 8
    tb = qp.shape[0]
    for h in range(n_heads):
        qh = qp[:, h * d_key:(h + 1) * d_key]
        qn = qh * lax.rsqrt(jnp.mean(qh * qh, axis=-1, keepdims=True) + EPS) * qg_ref[...]
        qb = qn.astype(BF16)
        s1 = lax.dot_general(keys_ref[2 * h], qb[:, :d_half], _NT, preferred_element_type=F32)
        s2 = lax.dot_general(keys_ref[2 * h + 1], qb[:, d_half:], _NT, preferred_element_type=F32)
        t1, i1, _ = _top_k_rows(s1, topk)
        t2, _, r2 = _top_k_rows(s2, topk)
        cand = jnp.concatenate(
            [t1[a] + t2[b] for a, b in pairs]
            + [jnp.full((n_rows - n_pairs, tb), NEG_INF, F32)], axis=0)
        iota = lax.broadcasted_iota(jnp.int32, cand.shape, 0).astype(F32)
        work = cand
        z = jnp.zeros((1, tb), F32)
        m = t1[0] + t2[0]
        for _ in range(topk):
            mx = jnp.max(work, axis=0, keepdims=True)
            ix = jnp.min(jnp.where(work == mx, iota, float(n_rows)), axis=0, keepdims=True)
            work = jnp.where(iota == ix, NEG_INF, work)
            z = z + jnp.exp(mx - m)
        chosen = jnp.where((work == NEG_INF) & (iota < float(n_pairs)), 1.0, 0.0)
        key_iota = lax.broadcasted_iota(jnp.int32, s1.shape, 0).astype(F32)
        n1 = jnp.zeros(s1.shape, F32)
        off = 0
        for a in range(topk):
            cnt = sum(1 for pa, _ in pairs if pa == a)
            n_a = jnp.sum(chosen[off:off + cnt], axis=0, keepdims=True)
            off += cnt
            n1 = jnp.where(key_iota == i1[a], n_a, n1)
        n1_ref[h] = n1.astype(n1_ref.dtype)
        e1_ref[h] = jnp.exp(s1 - t1[0]).astype(e1_ref.dtype)
        r2_ref[h] = pltpu.bitcast(r2.astype(BF16), r2_ref.dtype)
        e2_ref[h] = pltpu.bitcast((jnp.exp(s2 - t2[0]) / z).astype(BF16), e2_ref.dtype)


def _peer_select(x1, wq_bf, q_g, keys_bf, n_heads, d_key, n_keys):
    tp, d_model = x1.shape
    tb = SEL_TB
    row_spec = pl.BlockSpec((n_heads, n_keys, tb), lambda i: (0, 0, i))
    tile_spec = pl.BlockSpec((n_heads, n_keys // 2, tb), lambda i: (0, 0, i))
    out_shape = [jax.ShapeDtypeStruct((n_heads, n_keys, tp), F32)] * 2 + [
        jax.ShapeDtypeStruct((n_heads, n_keys // 2, tp), jnp.uint32)] * 2
    return pl.pallas_call(
        functools.partial(_peer_sel_kernel, n_heads=n_heads, d_key=d_key, topk=PEER_TOPK),
        grid=(tp // tb,),
        in_specs=[pl.BlockSpec((tb, d_model), lambda i: (i, 0)),
                  pl.BlockSpec(wq_bf.shape, lambda i: (0, 0)),
                  pl.BlockSpec((1, d_key), lambda i: (0, 0)),
                  pl.BlockSpec(keys_bf.shape, lambda i: (0, 0, 0))],
        out_specs=[row_spec, row_spec, tile_spec, tile_spec],
        out_shape=out_shape,
        compiler_params=_params("arbitrary"),
        name="peer_select",
    )(x1, wq_bf, q_g, keys_bf)


def _peer_dense_kernel(xt_ref, xtb_ref, u_ref, vt_ref, n1_ref, e1_ref, r2_ref, e2_ref,
                       g_ref, b_ref, o_ref, acc_ref, act_ref, wg_ref,
                       *, n_heads, n_keys, lane_tile, alpha):
    c = pl.program_id(1)
    n_chunks = pl.num_programs(1) - 1
    n_i = act_ref.shape[0] // n_keys
    tb = act_ref.shape[1]
    zero = jnp.zeros((n_keys, lane_tile), BF16)

    @pl.when(c == 0)
    def _():
        acc_ref[...] = jnp.zeros(acc_ref.shape, F32)
        wg_ref[1] = jnp.zeros(wg_ref.shape[1:], BF16)

    def row_tile(ref, h, il, cols):
        row = jnp.broadcast_to(ref[h, il:il + 1, cols], (BF16_SUBLANES, lane_tile)).astype(BF16)
        return jnp.tile(row, (n_keys // BF16_SUBLANES, 1))

    def pv_previous():
        acc_ref[...] += jnp.dot(vt_ref[...], wg_ref[(c + 1) % 2], preferred_element_type=F32)

    @pl.when(c < n_chunks)
    def _():
        pv_previous()
        act_ref[...] = jnp.dot(u_ref[...], xtb_ref[...], preferred_element_type=F32)
        cur = c % 2
        for tc in range(tb // lane_tile):
            cols = slice(tc * lane_tile, (tc + 1) * lane_tile)
            for il in range(n_i):
                w = zero
                for h in range(n_heads):
                    sel = pltpu.bitcast(r2_ref[h, :, cols], BF16) < row_tile(n1_ref, h, il, cols)
                    e2 = pltpu.bitcast(e2_ref[h, :, cols], BF16)
                    w = w + jnp.where(sel, e2, zero) * row_tile(e1_ref, h, il, cols)
                rows = slice(il * n_keys, (il + 1) * n_keys)
                wg_ref[cur, rows, cols] = w * _gelu(act_ref[rows, cols]).astype(BF16)

    @pl.when(c == n_chunks)
    def _():
        pv_previous()
        o_ref[...] = _layer_norm(alpha * xt_ref[...] + acc_ref[...], g_ref[...], b_ref[...], 0)


def _peer_dense(x1t, x1t_bf, u_bf, vt_bf, n1, e1, r2, e2, ln_g, ln_b, alpha):
    d_model, tp = x1t.shape
    n_heads, n_keys, _ = n1.shape
    n_exp = u_bf.shape[0]
    tb, ec = PEER_TB, PEER_EC
    n_i = ec // n_keys
    n_chunks = n_exp // ec
    tok = lambda t, c: (0, t)
    cur = lambda c: jnp.minimum(c, n_chunks - 1)
    prev = lambda c: jnp.maximum(c - 1, 0)
    return pl.pallas_call(
        functools.partial(_peer_dense_kernel, n_heads=n_heads, n_keys=n_keys,
                          lane_tile=PEER_LANE_TILE, alpha=alpha),
        grid=(tp // tb, n_chunks + 1),
        in_specs=[pl.BlockSpec((d_model, tb), tok),
                  pl.BlockSpec((d_model, tb), tok),
                  pl.BlockSpec((ec, d_model), lambda t, c: (cur(c), 0)),
                  pl.BlockSpec((d_model, ec), lambda t, c: (0, prev(c))),
                  pl.BlockSpec((n_heads, n_i, tb), lambda t, c: (0, cur(c), t)),
                  pl.BlockSpec((n_heads, n_i, tb), lambda t, c: (0, cur(c), t)),
                  pl.BlockSpec((n_heads, n_keys // 2, tb), lambda t, c: (0, 0, t)),
                  pl.BlockSpec((n_heads, n_keys // 2, tb), lambda t, c: (0, 0, t)),
                  pl.BlockSpec((d_model, 1), lambda t, c: (0, 0)),
                  pl.BlockSpec((d_model, 1), lambda t, c: (0, 0))],
        out_specs=pl.BlockSpec((d_model, tb), tok),
        out_shape=jax.ShapeDtypeStruct((d_model, tp), F32),
        scratch_shapes=[pltpu.VMEM((d_model, tb), F32), pltpu.VMEM((ec, tb), F32),
                        pltpu.VMEM((2, ec, tb), BF16)],
        compiler_params=_params("arbitrary", "arbitrary"),
        name="peer_dense",
    )(x1t, x1t_bf, u_bf, vt_bf, n1, e1, r2, e2, ln_g, ln_b)


def kernel(x_prompt, x_sample, cache_k, cache_v, state_ssm_re, state_ssm_im, page_table,
           w_in, lambda_q1, lambda_k1, lambda_q2, lambda_k2, attn_subln_g,
           ssm_lambda_re, ssm_lambda_im, ssm_log_dt, ssm_b_re, ssm_b_im, ssm_c_re, ssm_c_im,
           ssm_d, ssm_w_glu, ssm_b_glu, w_out, ln1_g, ln1_b,
           peer_w_q, peer_q_g, peer_sub_keys, peer_u, peer_v, ln2_g, ln2_b):
    depth = w_in.shape[0]
    batch, seq, d_model = x_prompt.shape
    dec_batch, dec_seq, _ = x_sample.shape
    _, n_phys, page, n_heads, _, dqk = cache_k.shape
    dv = cache_v.shape[-1]
    d_attn = n_heads * dv
    groups, p_dim = state_ssm_re.shape[2:]
    ch = ssm_b_re.shape[-1]
    peer_heads, _, n_keys, d_half = peer_sub_keys.shape[1:]
    d_key = 2 * d_half
    alpha = (2.0 * depth) ** 0.25
    scale = dqk ** -0.5

    n_prompt = batch * seq
    n_sample = dec_batch * dec_seq
    n_tok = n_prompt + n_sample
    tp = -(-n_tok // TOKEN_PAD) * TOKEN_PAD
    n_chunks = seq // SSM_CHUNK
    n_steps = int(math.log2(n_chunks))
    assert n_chunks == 1 << n_steps and seq % ATTN_TQ == 0 and n_prompt % TOKEN_PAD == 0

    x_all = jnp.concatenate(
        [x_prompt.reshape(n_prompt, d_model), x_sample.reshape(n_sample, d_model),
         jnp.zeros((tp - n_tok, d_model), x_prompt.dtype)], axis=0)
    slope_vals = 2.0 ** (-8.0 * jnp.arange(1, n_heads + 1, dtype=F32) / n_heads)
    slopes = jnp.broadcast_to(slope_vals[:, None, None], (n_heads, 1, 128))
    slopecol = jnp.repeat(slope_vals, 2 * dec_seq)[:, None]
    eye_hc = jnp.eye(2 * n_heads, dtype=F32)

    outs = {name: [] for name in ("kp", "vp", "hpr", "hpi", "ks", "vs", "hsr", "hsi")}
    for l in range(depth):
        lam_init = 0.8 - 0.6 * math.exp(-0.3 * l)
        lamv = jnp.stack([lambda_q1[l], lambda_k1[l], lambda_q2[l], lambda_k2[l]]).astype(F32)
        subln_g = attn_subln_g[l].astype(F32)[None, :]

        w_bf = w_in[l].astype(BF16)
        kt_prompt, k_last, v_all, u_all, qkv_bf = _project(
            x_all, w_bf, w_bf[:, d_attn:2 * d_attn].T, d_attn, scale, batch, seq)

        a_prompt = _attn_prompt(qkv_bf, lamv, subln_g, slopes, batch, seq, n_heads, dv, dqk,
                                lam_init)

        qkv_s = qkv_bf[n_prompt:n_tok].reshape(dec_batch, dec_seq, 3 * d_attn)
        q_s = qkv_s[..., :d_attn].reshape(dec_batch, dec_seq, 2 * n_heads, dqk)
        qbd = (q_s.transpose(0, 2, 1, 3)[:, :, :, None, :].astype(F32)
               * eye_hc[None, :, None, :, None]).astype(BF16)
        qbd = qbd.reshape(dec_batch, 2 * n_heads * dec_seq, d_attn)
        k_new = jnp.pad(qkv_s[..., d_attn:2 * d_attn].transpose(0, 2, 1),
                        ((0, 0), (0, 0), (0, page - dec_seq)))
        v_new = jnp.pad(qkv_s[..., 2 * d_attn:], ((0, 0), (0, page - dec_seq), (0, 0)))
        kt2d = jnp.transpose(cache_k[l], (0, 2, 3, 4, 1)).reshape(n_phys * d_attn, page)
        a_s = _attn_sample(page_table, lamv, subln_g, slopecol, qbd, kt2d,
                           cache_v[l].reshape(n_phys * page * n_heads, dv),
                           k_new, v_new, n_heads, dec_seq, lam_init)
        a_sample = a_s.reshape(dec_batch, n_heads, dec_seq, dv).transpose(0, 2, 1, 3)
        a_all = jnp.concatenate(
            [a_prompt, a_sample.reshape(n_sample, d_attn),
             jnp.zeros((tp - n_tok, d_attn), BF16)], axis=0)

        ssm_p = (ssm_lambda_re[l], ssm_lambda_im[l], ssm_log_dt[l], ssm_b_re[l], ssm_b_im[l],
                 ssm_c_re[l], ssm_c_im[l], ssm_d[l])
        tm, sm, cm, dsk, coef = _ssm_tables(*ssm_p, SSM_CHUNK, n_steps)
        u_r = (u_all[:n_prompt].reshape(batch, n_chunks, SSM_CHUNK, groups, ch)
               .transpose(0, 3, 1, 2, 4).reshape(batch, groups, n_chunks, SSM_CHUNK * ch))
        y_r, h_p = _ssm_prompt(u_r, tm, sm, cm, dsk, coef)
        yg_prompt = (y_r.reshape(batch, groups, n_chunks, SSM_CHUNK, ch)
                     .transpose(0, 2, 3, 1, 4).reshape(n_prompt, groups * ch))

        tm_s, sm_s, cm_s, dsk_s, coef_s = _ssm_tables(*ssm_p, dec_seq, 1)
        us_r = (u_all[n_prompt:n_tok].reshape(dec_batch, dec_seq, groups, ch)
                .transpose(2, 0, 1, 3).reshape(groups, dec_batch, dec_seq * ch))
        h0r = state_ssm_re[l].astype(F32).transpose(1, 0, 2)
        h0i = state_ssm_im[l].astype(F32).transpose(1, 0, 2)
        ys_r, h_s = _ssm_sample(us_r, jnp.concatenate([h0r, h0i], -1),
                                jnp.concatenate([h0i, h0r], -1),
                                tm_s, sm_s[..., :2 * p_dim], cm_s, dsk_s, coef_s)
        yg_sample = (ys_r.reshape(groups, dec_batch, dec_seq, ch)
                     .transpose(1, 2, 0, 3).reshape(n_sample, groups * ch))
        yg_all = jnp.concatenate(
            [yg_prompt, yg_sample, jnp.zeros((tp - n_tok, groups * ch), F32)], axis=0)

        x1 = _out_proj(x_all, a_all, yg_all, ssm_w_glu[l].astype(BF16),
                       ssm_b_glu[l].astype(F32)[None, :], w_out[l].astype(BF16),
                       ln1_g[l].astype(F32)[None, :], ln1_b[l].astype(F32)[None, :], alpha)

        keys_bf = peer_sub_keys[l].reshape(2 * peer_heads, n_keys, d_half).astype(BF16)
        n1, e1, r2, e2 = _peer_select(x1, peer_w_q[l].astype(BF16),
                                      peer_q_g[l].astype(F32)[None, :], keys_bf,
                                      peer_heads, d_key, n_keys)
        x1t = x1.T
        out_t = _peer_dense(x1t, x1t.astype(BF16), peer_u[l].astype(BF16),
                            peer_v[l].T.astype(BF16), n1, e1, r2, e2,
                            ln2_g[l].astype(F32)[:, None], ln2_b[l].astype(F32)[:, None], alpha)
        x_all = out_t.T

        outs["kp"].append(kt_prompt.reshape(batch, n_heads, 2, dqk, seq).transpose(0, 4, 1, 2, 3))
        outs["vp"].append(v_all[:n_prompt].reshape(batch, seq, n_heads, dv))
        outs["hpr"].append(h_p[:, :, 0, :p_dim])
        outs["hpi"].append(h_p[:, :, 0, p_dim:])
        outs["ks"].append(k_last[:n_sample].reshape(dec_batch, dec_seq, n_heads, 2, dqk))
        outs["vs"].append(v_all[n_prompt:n_tok].reshape(dec_batch, dec_seq, n_heads, dv))
        outs["hsr"].append(h_s[:, :, :p_dim].transpose(1, 0, 2))
        outs["hsi"].append(h_s[:, :, p_dim:].transpose(1, 0, 2))

    y_prompt = x_all[:n_prompt].reshape(batch, seq, d_model)
    y_sample = x_all[n_prompt:n_tok].reshape(dec_batch, dec_seq, d_model)
    st = lambda name: jnp.stack(outs[name])
    return (y_prompt, y_sample, st("kp"), st("vp"), st("hpr"), st("hpi"),
            st("ks"), st("vs"), st("hsr"), st("hsi"))
```

```python
import functools
import math

import jax
import jax.numpy as jnp
from jax import lax
from jax.experimental import pallas as pl
from jax.experimental.pallas import tpu as pltpu

F32 = jnp.float32
BF16 = jnp.bfloat16

EPS = 1e-5
MASK_VALUE = -1e30
PEER_TOPK = 16
NEG_INF = float("-inf")

TOKEN_PAD = 512
PROJ_TM = 512
ATTN_TQ = 1024
ATTN_TK = 1024
ATTN_RS = 512
SAMPLE_PAGES = 16
SSM_CHUNK = 16
SEL_TB = 256
PEER_TB = 512
PEER_EC = 1024
BF16_SUBLANES = 16
PEER_LANE_TILE = 128
VMEM_LIMIT = 56 * 1024 * 1024

_NT = (((1,), (1,)), ((), ()))
_HI = lax.Precision.HIGHEST


def _params(*sem):
    return pltpu.CompilerParams(dimension_semantics=sem, vmem_limit_bytes=VMEM_LIMIT)


def _gelu(x):
    return 0.5 * x * (1.0 + lax.erf(x * (2.0 ** -0.5)))


def _proj_kernel(x_ref, w_ref, wkt_ref, kt_ref, ks_ref, vp_ref, vs_ref, u_ref, qkv_ref,
                 *, d_attn, n_heads, scale, n_prompt_blocks):
    i = pl.program_id(0)
    x = x_ref[...].astype(BF16)
    z = jnp.dot(x, w_ref[...], preferred_element_type=F32)
    k = z[:, d_attn:2 * d_attn]
    v = z[:, 2 * d_attn:3 * d_attn]
    u_ref[...] = z[:, 3 * d_attn:]
    qkv_ref[:, :d_attn] = (z[:, :d_attn] * scale).astype(BF16)
    qkv_ref[:, d_attn:2 * d_attn] = k.astype(BF16)
    qkv_ref[:, 2 * d_attn:] = v.astype(BF16)
    tm = x.shape[0]
    dv = d_attn // n_heads

    def store_v(ref):
        for h in range(n_heads):
            ref[pl.ds(h, tm, stride=n_heads), :] = v[:, h * dv:(h + 1) * dv]

    @pl.when(i < n_prompt_blocks)
    def _():
        kt_ref[...] = lax.dot_general(wkt_ref[...], x, _NT, preferred_element_type=F32)
        store_v(vp_ref)

    @pl.when(i == pl.num_programs(0) - 1)
    def _():
        ks_ref[...] = k
        store_v(vs_ref)


def _project(x_all, w_bf, wkt_bf, d_attn, n_heads, scale, batch, seq):
    tp, d_model = x_all.shape
    n_out = w_bf.shape[1]
    d_u = n_out - 3 * d_attn
    dv = d_attn // n_heads
    tm = PROJ_TM
    per_seq = seq // tm
    n_prompt_blocks = batch * per_seq
    assert tp // tm == n_prompt_blocks + 1
    last = n_prompt_blocks - 1
    row = lambda i: (i, 0)
    kt_map = lambda i: (jnp.minimum(i, last) // per_seq, 0, jnp.minimum(i, last) % per_seq)
    return pl.pallas_call(
        functools.partial(_proj_kernel, d_attn=d_attn, n_heads=n_heads, scale=scale,
                          n_prompt_blocks=n_prompt_blocks),
        grid=(tp // tm,),
        in_specs=[pl.BlockSpec((tm, d_model), row),
                  pl.BlockSpec((d_model, n_out), lambda i: (0, 0)),
                  pl.BlockSpec((d_attn, d_model), lambda i: (0, 0))],
        out_specs=[pl.BlockSpec((None, d_attn, tm), kt_map),
                   pl.BlockSpec((tm, d_attn), lambda i: (0, 0)),
                   pl.BlockSpec((tm * n_heads, dv), lambda i: (jnp.minimum(i, last), 0)),
                   pl.BlockSpec((tm * n_heads, dv), lambda i: (0, 0)),
                   pl.BlockSpec((tm, d_u), row),
                   pl.BlockSpec((tm, 3 * d_attn), row)],
        out_shape=[jax.ShapeDtypeStruct((batch, d_attn, seq), F32),
                   jax.ShapeDtypeStruct((tm, d_attn), F32),
                   jax.ShapeDtypeStruct((batch * seq * n_heads, dv), F32),
                   jax.ShapeDtypeStruct((tm * n_heads, dv), F32),
                   jax.ShapeDtypeStruct((tp, d_u), F32),
                   jax.ShapeDtypeStruct((tp, 3 * d_attn), BF16)],
        compiler_params=_params("arbitrary"),
        name="proj",
    )(x_all, w_bf, wkt_bf)


def _lam_value(lamv_ref, lam_init):
    lv = lamv_ref[...]
    e1 = jnp.exp(jnp.sum(lv[0:1] * lv[1:2], axis=1, keepdims=True))
    e2 = jnp.exp(jnp.sum(lv[2:3] * lv[3:4], axis=1, keepdims=True))
    return e1 - e2 + lam_init


def _subln(o, g, lam_init):
    ms = jnp.mean(o * o, axis=-1, keepdims=True)
    return o * lax.rsqrt(ms + EPS) * g * (1.0 - lam_init)


def _attn_prompt_kernel(qi_ref, kj_ref, lamv_ref, g_ref, slope_ref, q_ref, k_ref, v_ref, o_ref,
                        m_ref, l_ref, acc_ref, *, tq, tk, rs, dqk, lam_init):
    t = pl.program_id(2)
    i = qi_ref[t]
    j = kj_ref[t]
    lanes = m_ref.shape[-1]
    ratio = tq // tk
    off = j - ratio * i

    @pl.when(j == 0)
    def _():
        m_ref[...] = jnp.full(m_ref.shape, MASK_VALUE, F32)
        l_ref[...] = jnp.zeros(l_ref.shape, F32)
        acc_ref[...] = jnp.zeros(acc_ref.shape, F32)

    n_sub = tq // rs

    def step(shift):
        kcol = lax.broadcasted_iota(jnp.int32, (1, tk), 1) + (j * tk - i * tq)
        colbias = slope_ref[...][:, :1] * kcol.astype(F32)

        def n_keys(r):
            return tk if shift is None else max(0, min(tk, (r + 1) * rs - shift))

        def masked(r):
            return shift is not None and 0 < (r + 1) * rs - shift <= tk

        def scores(r):
            q = q_ref[r * rs:(r + 1) * rs, :]
            lane = lax.broadcasted_iota(jnp.int32, q.shape, 1)
            zero = jnp.zeros_like(q)
            q2 = jnp.concatenate([jnp.where(lane < dqk, q, zero), jnp.where(lane < dqk, zero, q)], axis=0)
            return lax.dot_general(q2, k_ref[:n_keys(r), :], _NT, preferred_element_type=F32)

        blocks = [r for r in range(n_sub) if n_keys(r) > 0]
        ahead = 2
        pending = {r: scores(r) for r in blocks[:ahead]}
        for n, r in enumerate(blocks):
            if n + ahead < len(blocks):
                pending[blocks[n + ahead]] = scores(blocks[n + ahead])
            nk = n_keys(r)
            rows = slice(2 * r * rs, 2 * (r + 1) * rs)
            s = pending.pop(r) + colbias[:, :nk]
            if masked(r):
                row = lax.broadcasted_iota(jnp.int32, (2 * rs, nk), 0) & (rs - 1)
                col = lax.broadcasted_iota(jnp.int32, (2 * rs, nk), 1)
                s = jnp.where(row + (r * rs - shift) >= col, s, MASK_VALUE)
            m_prev = m_ref[rows, :]
            m_new = jnp.maximum(m_prev, jnp.max(s, axis=1, keepdims=True))
            alpha = jnp.exp(m_prev - m_new)
            p = jnp.exp(s - jnp.tile(m_new, (1, nk // lanes)))
            l_ref[rows, :] = alpha * l_ref[rows, :] + jnp.sum(p, axis=1, keepdims=True)
            acc_ref[rows, :] = alpha * acc_ref[rows, :] + jnp.dot(
                p.astype(BF16), v_ref[:nk, :], preferred_element_type=F32)
            m_ref[rows, :] = m_new

    @pl.when(off < 0)
    def _():
        step(None)

    for d in range(ratio):
        pl.when(off == d)(functools.partial(step, d * tk))

    @pl.when(off == ratio - 1)
    def _():
        lam = _lam_value(lamv_ref, lam_init)
        for r in range(n_sub):
            r1 = slice(2 * r * rs, (2 * r + 1) * rs)
            r2 = slice((2 * r + 1) * rs, 2 * (r + 1) * rs)
            o = acc_ref[r1, :] / l_ref[r1, :] - lam * (acc_ref[r2, :] / l_ref[r2, :])
            o_ref[r * rs:(r + 1) * rs, :] = _subln(o, g_ref[...], lam_init).astype(o_ref.dtype)


def _attn_prompt(qkv_bf, lamv, subln_g, slopes, batch, seq, n_heads, dv, dqk, lam_init):
    tq, tk = ATTN_TQ, ATTN_TK
    nq, nk = seq // tq, seq // tk
    ratio = tq // tk
    d_attn = n_heads * dv
    pairs = [(i, j) for i in range(nq) for j in range(ratio * (i + 1))]
    qi = jnp.asarray([p[0] for p in pairs], jnp.int32)
    kj = jnp.asarray([p[1] for p in pairs], jnp.int32)
    grid_spec = pltpu.PrefetchScalarGridSpec(
        num_scalar_prefetch=2,
        grid=(batch, n_heads, len(pairs)),
        in_specs=[pl.BlockSpec(lamv.shape, lambda b, h, t, qi, kj: (0, 0)),
                  pl.BlockSpec((1, dv), lambda b, h, t, qi, kj: (0, 0)),
                  pl.BlockSpec((None, 1, 128), lambda b, h, t, qi, kj: (h, 0, 0)),
                  pl.BlockSpec((tq, dv), lambda b, h, t, qi, kj: (b * nq + qi[t], h)),
                  pl.BlockSpec((tk, dv), lambda b, h, t, qi, kj: (b * nk + kj[t], n_heads + h)),
                  pl.BlockSpec((tk, dv), lambda b, h, t, qi, kj: (b * nk + kj[t], 2 * n_heads + h))],
        out_specs=pl.BlockSpec((tq, dv), lambda b, h, t, qi, kj: (b * nq + qi[t], h)),
        scratch_shapes=[pltpu.VMEM((2 * tq, dv), F32), pltpu.VMEM((2 * tq, dv), F32),
                        pltpu.VMEM((2 * tq, dv), F32)],
    )
    return pl.pallas_call(
        functools.partial(_attn_prompt_kernel, tq=tq, tk=tk, rs=ATTN_RS, dqk=dqk, lam_init=lam_init),
        grid_spec=grid_spec,
        out_shape=jax.ShapeDtypeStruct((batch * seq, d_attn), BF16),
        compiler_params=_params("arbitrary", "arbitrary", "arbitrary"),
        name="attn_prompt",
    )(qi, kj, lamv, subln_g, slopes, qkv_bf, qkv_bf, qkv_bf)


def _attn_sample_kernel(pt_ref, lamv_ref, g_ref, slopecol_ref, qbd_ref, kn_ref, vn_ref, *rest,
                        pp, page, past_len, n_heads, n_new, lam_init):
    del pt_ref
    k_refs, v_refs = rest[:pp], rest[pp:2 * pp]
    o_ref, m_ref, l_ref, acc_ref = rest[2 * pp:]
    step = pl.program_id(1)
    rows_h = 2 * n_new
    dv = m_ref.shape[1]

    @pl.when(step == 0)
    def _():
        m_ref[...] = jnp.full(m_ref.shape, MASK_VALUE, F32)
        l_ref[...] = jnp.zeros(l_ref.shape, F32)
        acc_ref[...] = jnp.zeros(acc_ref.shape, F32)

    def update(s, v):
        m_prev = m_ref[...]
        m_new = jnp.maximum(m_prev, jnp.max(s, axis=1, keepdims=True))
        alpha = jnp.exp(m_prev - m_new)
        p = jnp.exp(s - jnp.tile(m_new, (1, s.shape[1] // dv)))
        l_ref[...] = alpha * l_ref[...] + jnp.sum(p, axis=1, keepdims=True)
        acc_ref[...] = jnp.tile(alpha, (1, n_heads)) * acc_ref[...] + jnp.dot(
            p.astype(BF16), v, preferred_element_type=F32)
        m_ref[...] = m_new

    def page_values(n):
        return jnp.concatenate(
            [v_refs[n][pl.ds(h, page, stride=n_heads), :] for h in range(n_heads)], axis=1).astype(BF16)

    slope = slopecol_ref[...]
    kt = jnp.concatenate([k_refs[n][...].astype(BF16) for n in range(pp)], axis=1)
    kpos = lax.broadcasted_iota(jnp.int32, (1, pp * page), 1) + (step * (pp * page) - past_len)
    s = jnp.dot(qbd_ref[...], kt, preferred_element_type=F32) + slope * kpos.astype(F32)
    update(s, jnp.concatenate([page_values(n) for n in range(pp)], axis=0))

    @pl.when(step == pl.num_programs(1) - 1)
    def _():
        s = jnp.dot(qbd_ref[...], kn_ref[...], preferred_element_type=F32)
        col = lax.broadcasted_iota(jnp.int32, s.shape, 1)
        qi = lax.broadcasted_iota(jnp.int32, s.shape, 0) & (n_new - 1)
        s = s + slope * col.astype(F32)
        update(jnp.where(col <= qi, s, MASK_VALUE), vn_ref[...])
        lam = _lam_value(lamv_ref, lam_init)
        for h in range(n_heads):
            r = slice(h * rows_h, (h + 1) * rows_h)
            a = acc_ref[r, h * dv:(h + 1) * dv] / l_ref[r, :]
            o = a[:n_new] - lam * a[n_new:]
            o_ref[h * n_new:(h + 1) * n_new, :] = _subln(o, g_ref[...], lam_init).astype(o_ref.dtype)


def _attn_sample(page_table, lamv, subln_g, slopecol, qbd, kt2d, v2d, k_new, v_new,
                 n_heads, n_new, lam_init):
    dec_batch, n_pages = page_table.shape
    page, dv = kt2d.shape[1], v2d.shape[1]
    n_rows, d_attn = qbd.shape[1:]
    pp = SAMPLE_PAGES
    assert n_pages % pp == 0
    pt_flat = page_table.reshape(-1)
    fixed = lambda b, s, pt: (0, 0)
    seq3 = lambda b, s, pt: (b, 0, 0)

    def page_map(n):
        return lambda b, s, pt: (pt[b * n_pages + s * pp + n], 0)

    kernel = functools.partial(
        _attn_sample_kernel, pp=pp, page=page, past_len=n_pages * page,
        n_heads=n_heads, n_new=n_new, lam_init=lam_init)
    grid_spec = pltpu.PrefetchScalarGridSpec(
        num_scalar_prefetch=1,
        grid=(dec_batch, n_pages // pp),
        in_specs=[pl.BlockSpec(lamv.shape, fixed),
                  pl.BlockSpec((1, dv), fixed),
                  pl.BlockSpec((n_rows, 1), fixed),
                  pl.BlockSpec((None, n_rows, d_attn), seq3),
                  pl.BlockSpec((None,) + k_new.shape[1:], seq3),
                  pl.BlockSpec((None,) + v_new.shape[1:], seq3)]
        + [pl.BlockSpec((d_attn, page), page_map(n)) for n in range(pp)]
        + [pl.BlockSpec((page * n_heads, dv), page_map(n)) for n in range(pp)],
        out_specs=pl.BlockSpec((None, n_heads * n_new, dv), seq3),
        scratch_shapes=[pltpu.VMEM((n_rows, dv), F32), pltpu.VMEM((n_rows, dv), F32),
                        pltpu.VMEM((n_rows, n_heads * dv), F32)],
    )
    return pl.pallas_call(
        kernel,
        grid_spec=grid_spec,
        out_shape=jax.ShapeDtypeStruct((dec_batch, n_heads * n_new, dv), BF16),
        compiler_params=_params("arbitrary", "arbitrary"),
        name="attn_sample",
    )(pt_flat, lamv, subln_g, slopecol, qbd, k_new, v_new, *([kt2d] * pp), *([v2d] * pp))


def _ssm_prompt_kernel(u_ref, tm_ref, sm_ref, cm_ref, d_ref, coef_ref, y_ref, h_ref,
                       *, n_chunks, n_steps):
    u = u_ref[...]
    x = jnp.dot(u, sm_ref[...], precision=_HI, preferred_element_type=F32)
    half = x.shape[1] // 2
    x0, x1 = x[:, :half], x[:, half:]
    row = lax.broadcasted_iota(jnp.int32, x0.shape, 0)

    def shifted(a, d):
        return jnp.where(row >= d, pltpu.roll(a, d, 0), 0.0)

    y0, y1 = shifted(x0, 1), shifted(x1, 1)
    coef = coef_ref[...]
    for s in range(n_steps):
        d = 1 << s
        p = coef[2 * s:2 * s + 1]
        q = coef[2 * s + 1:2 * s + 2]
        s0, s1 = shifted(y0, d), shifted(y1, d)
        y0, y1 = y0 + p * s0 + q * s1, y1 + p * s1 - q * s0
    last = n_chunks - 1
    h_ref[...] = (coef[0:1] * y0[last:last + 1] + coef[1:2] * y1[last:last + 1]
                  + x0[last:last + 1])
    y = (jnp.dot(u, tm_ref[...], precision=_HI, preferred_element_type=F32)
         + jnp.dot(y0, cm_ref[...], precision=_HI, preferred_element_type=F32)
         + u * d_ref[...])
    y_ref[...] = _gelu(y)


def _ssm_prompt(u_r, tm, sm, cm, dsk, coef):
    batch, groups, n_chunks, width = u_r.shape
    n_steps = coef.shape[1] // 2
    state = cm.shape[1]
    return pl.pallas_call(
        functools.partial(_ssm_prompt_kernel, n_chunks=n_chunks, n_steps=n_steps),
        grid=(batch, groups),
        in_specs=[pl.BlockSpec((None, None, n_chunks, width), lambda b, g: (b, g, 0, 0)),
                  pl.BlockSpec((None, width, width), lambda b, g: (g, 0, 0)),
                  pl.BlockSpec((None, width, 2 * state), lambda b, g: (g, 0, 0)),
                  pl.BlockSpec((None, state, width), lambda b, g: (g, 0, 0)),
                  pl.BlockSpec((None, 1, width), lambda b, g: (g, 0, 0)),
                  pl.BlockSpec((None, 2 * n_steps, state), lambda b, g: (g, 0, 0))],
        out_specs=[pl.BlockSpec((None, None, n_chunks, width), lambda b, g: (b, g, 0, 0)),
                   pl.BlockSpec((None, None, 1, state), lambda b, g: (b, g, 0, 0))],
        out_shape=[jax.ShapeDtypeStruct((batch, groups, n_chunks, width), F32),
                   jax.ShapeDtypeStruct((batch, groups, 1, state), F32)],
        compiler_params=_params("arbitrary", "arbitrary"),
        name="ssm_prompt",
    )(u_r, tm, sm, cm, dsk, coef)


def _ssm_sample_kernel(u_ref, h0_ref, h0s_ref, tm_ref, sm_ref, cm_ref, d_ref, coef_ref,
                       y_ref, h_ref):
    u = u_ref[...]
    h0 = h0_ref[...]
    coef = coef_ref[...]
    x = jnp.dot(u, sm_ref[...], precision=_HI, preferred_element_type=F32)
    h_ref[...] = coef[0:1] * h0 + coef[1:2] * h0s_ref[...] + x
    y = (jnp.dot(u, tm_ref[...], precision=_HI, preferred_element_type=F32)
         + jnp.dot(h0, cm_ref[...], precision=_HI, preferred_element_type=F32)
         + u * d_ref[...])
    y_ref[...] = _gelu(y)


def _ssm_sample(u_r, h0, h0s, tm, sm, cm, dsk, coef):
    groups, seqs, width = u_r.shape
    state = h0.shape[2]
    return pl.pallas_call(
        _ssm_sample_kernel,
        grid=(groups,),
        in_specs=[pl.BlockSpec((None, seqs, width), lambda g: (g, 0, 0)),
                  pl.BlockSpec((None, seqs, state), lambda g: (g, 0, 0)),
                  pl.BlockSpec((None, seqs, state), lambda g: (g, 0, 0)),
                  pl.BlockSpec((None, width, width), lambda g: (g, 0, 0)),
                  pl.BlockSpec((None, width, state), lambda g: (g, 0, 0)),
                  pl.BlockSpec((None, state, width), lambda g: (g, 0, 0)),
                  pl.BlockSpec((None, 1, width), lambda g: (g, 0, 0)),
                  pl.BlockSpec((None, 2, state), lambda g: (g, 0, 0))],
        out_specs=[pl.BlockSpec((None, seqs, width), lambda g: (g, 0, 0)),
                   pl.BlockSpec((None, seqs, state), lambda g: (g, 0, 0))],
        out_shape=[jax.ShapeDtypeStruct((groups, seqs, width), F32),
                   jax.ShapeDtypeStruct((groups, seqs, state), F32)],
        compiler_params=_params("arbitrary"),
        name="ssm_sample",
    )(u_r, h0, h0s, tm, sm, cm, dsk, coef)


def _ssm_tables(lam_re, lam_im, log_dt, b_re, b_im, c_re, c_im, d_skip, chunk, n_steps):
    groups, p_dim, ch = b_re.shape
    lr, li = lam_re.astype(F32), lam_im.astype(F32)
    dt = jnp.exp(log_dt.astype(F32))[:, None]
    mag = jnp.exp(lr * dt)
    a_re, a_im = mag * jnp.cos(li * dt), mag * jnp.sin(li * dt)
    den = lr * lr + li * li
    n_re, n_im = a_re - 1.0, a_im
    g_re = (n_re * lr + n_im * li) / den
    g_im = (n_im * lr - n_re * li) / den
    br, bi = b_re.astype(F32), b_im.astype(F32)
    bb_re = g_re[..., None] * br - g_im[..., None] * bi
    bb_im = g_re[..., None] * bi + g_im[..., None] * br

    def power(n):
        n = n.astype(F32)[:, None, None]
        m = jnp.exp(lr * dt * n)
        return m * jnp.cos(li * dt * n), m * jnp.sin(li * dt * n)

    taus = jnp.arange(chunk + 1)
    pw_re, pw_im = power(taus)
    ab_re = pw_re[:chunk, :, :, None] * bb_re - pw_im[:chunk, :, :, None] * bb_im
    ab_im = pw_re[:chunk, :, :, None] * bb_im + pw_im[:chunk, :, :, None] * bb_re
    cr, ci = c_re.astype(F32), c_im.astype(F32)
    kern = (jnp.einsum('gcp,tgpd->gtcd', cr, ab_re, precision=_HI)
            - jnp.einsum('gcp,tgpd->gtcd', ci, ab_im, precision=_HI))
    tm = jnp.stack([jnp.pad(kern[:, :chunk - s], ((0, 0), (s, 0), (0, 0), (0, 0)))
                    for s in range(chunk)], axis=1)
    tm = tm.transpose(0, 1, 4, 2, 3).reshape(groups, chunk * ch, chunk * ch)
    sm_re = ab_re[::-1].transpose(1, 0, 3, 2).reshape(groups, chunk * ch, p_dim)
    sm_im = ab_im[::-1].transpose(1, 0, 3, 2).reshape(groups, chunk * ch, p_dim)
    sm = jnp.concatenate([sm_re, sm_im, sm_im, sm_re], axis=-1)
    ca_re = cr[:, None] * pw_re[1:].transpose(1, 0, 2)[:, :, None, :] \
        - ci[:, None] * pw_im[1:].transpose(1, 0, 2)[:, :, None, :]
    ca_im = cr[:, None] * pw_im[1:].transpose(1, 0, 2)[:, :, None, :] \
        + ci[:, None] * pw_re[1:].transpose(1, 0, 2)[:, :, None, :]
    cm = jnp.concatenate([ca_re, -ca_im], axis=-1)
    cm = cm.transpose(0, 3, 1, 2).reshape(groups, 2 * p_dim, chunk * ch)
    dsk = jnp.tile(d_skip.astype(F32), (1, chunk)).reshape(groups, 1, chunk * ch)
    steps = chunk * (2 ** jnp.arange(n_steps))
    sp_re, sp_im = power(steps)
    p_rows = jnp.concatenate([sp_re, sp_re], axis=-1)
    q_rows = jnp.concatenate([-sp_im, sp_im], axis=-1)
    coef = jnp.stack([p_rows, q_rows], axis=1).reshape(2 * n_steps, groups, 2 * p_dim)
    return tm, sm, cm, dsk, coef.transpose(1, 0, 2)


def _layer_norm(y, g, b, axis):
    mu = jnp.mean(y, axis=axis, keepdims=True)
    var = jnp.mean(jnp.square(y - mu), axis=axis, keepdims=True)
    return (y - mu) * lax.rsqrt(var + EPS) * g + b


def _out_kernel(x_ref, a_ref, yg_ref, wglu_ref, bglu_ref, wo_ref, g_ref, b_ref, x1_ref,
                *, d_attn, alpha):
    yg = yg_ref[...]
    gate = jax.nn.sigmoid(
        jnp.dot(yg.astype(BF16), wglu_ref[...], preferred_element_type=F32) + bglu_ref[...])
    s = (yg * gate).astype(BF16)
    h = (jnp.dot(a_ref[...], wo_ref[:d_attn, :], preferred_element_type=F32)
         + jnp.dot(s, wo_ref[d_attn:, :], preferred_element_type=F32))
    x1_ref[...] = _layer_norm(alpha * x_ref[...] + h, g_ref[...], b_ref[...], -1)


def _out_proj(x_all, a_all, yg_all, wglu_bf, bglu, wo_bf, ln_g, ln_b, alpha):
    tp, d_model = x_all.shape
    d_attn = a_all.shape[1]
    d_ssm = yg_all.shape[1]
    tm = PROJ_TM
    row = lambda i: (i, 0)
    fixed = lambda i: (0, 0)
    return pl.pallas_call(
        functools.partial(_out_kernel, d_attn=d_attn, alpha=alpha),
        grid=(tp // tm,),
        in_specs=[pl.BlockSpec((tm, d_model), row), pl.BlockSpec((tm, d_attn), row),
                  pl.BlockSpec((tm, d_ssm), row), pl.BlockSpec((d_ssm, d_ssm), fixed),
                  pl.BlockSpec((1, d_ssm), fixed), pl.BlockSpec((d_model, d_model), fixed),
                  pl.BlockSpec((1, d_model), fixed), pl.BlockSpec((1, d_model), fixed)],
        out_specs=pl.BlockSpec((tm, d_model), row),
        out_shape=jax.ShapeDtypeStruct((tp, d_model), F32),
        compiler_params=_params("arbitrary"),
        name="out_proj",
    )(x_all, a_all, yg_all, wglu_bf, bglu, wo_bf, ln_g, ln_b)


def _top_k_rows(s, k):
    n = s.shape[0]
    iota = lax.broadcasted_iota(jnp.int32, s.shape, 0).astype(F32)
    rank = jnp.full(s.shape, float(k), F32)
    vals, idxs = [], []
    for r in range(k):
        mx = jnp.max(s, axis=0, keepdims=True)
        ix = jnp.min(jnp.where(s == mx, iota, float(n)), axis=0, keepdims=True)
        hit = iota == ix
        rank = jnp.where(hit, float(r), rank)
        s = jnp.where(hit, NEG_INF, s)
        vals.append(mx)
        idxs.append(ix)
    return vals, idxs, rank


def _staircase(k):
    return [(a, b) for a in range(k) for b in range(k) if (a + 1) * (b + 1) <= k]


def _peer_sel_kernel(x1_ref, wq_ref, qg_ref, keys_ref, n1_ref, e1_ref, r2_ref, e2_ref,
                     *, n_heads, d_key, topk):
    qp = jnp.dot(x1_ref[...].astype(BF16), wq_ref[...], preferred_element_type=F32)
    d_half = d_key // 2
    pairs = _staircase(topk)
    n_pairs = len(pairs)
    n_rows = -(-n_pairs // 8) * 8
    tb = qp.shape[0]
    for h in range(n_heads):
        qh = qp[:, h * d_key:(h + 1) * d_key]
        qn = qh * lax.rsqrt(jnp.mean(qh * qh, axis=-1, keepdims=True) + EPS) * qg_ref[...]
        qb = qn.astype(BF16)
        s1 = lax.dot_general(keys_ref[2 * h], qb[:, :d_half], _NT, preferred_element_type=F32)
        s2 = lax.dot_general(keys_ref[2 * h + 1], qb[:, d_half:], _NT, preferred_element_type=F32)
        t1, i1, _ = _top_k_rows(s1, topk)
        t2, _, r2 = _top_k_rows(s2, topk)
        cand = jnp.concatenate(
            [t1[a] + t2[b] for a, b in pairs]
            + [jnp.full((n_rows - n_pairs, tb), NEG_INF, F32)], axis=0)
        iota = lax.broadcasted_iota(jnp.int32, cand.shape, 0).astype(F32)
        work = cand
        z = jnp.zeros((1, tb), F32)
        m = t1[0] + t2[0]
        for _ in range(topk):
            mx = jnp.max(work, axis=0, keepdims=True)
            ix = jnp.min(jnp.where(work == mx, iota, float(n_rows)), axis=0, keepdims=True)
            work = jnp.where(iota == ix, NEG_INF, work)
            z = z + jnp.exp(mx - m)
        chosen = jnp.where((work == NEG_INF) & (iota < float(n_pairs)), 1.0, 0.0)
        key_iota = lax.broadcasted_iota(jnp.int32, s1.shape, 0).astype(F32)
        n1 = jnp.zeros(s1.shape, F32)
        off = 0
        for a in range(topk):
            cnt = sum(1 for pa, _ in pairs if pa == a)
            n_a = jnp.sum(chosen[off:off + cnt], axis=0, keepdims=True)
            off += cnt
            n1 = jnp.where(key_iota == i1[a], n_a, n1)
        n1_ref[h] = n1.astype(n1_ref.dtype)
        e1_ref[h] = jnp.exp(s1 - t1[0]).astype(e1_ref.dtype)
        r2_ref[h] = pltpu.bitcast(r2.astype(BF16), r2_ref.dtype)
        e2_ref[h] = pltpu.bitcast((jnp.exp(s2 - t2[0]) / z).astype(BF16), e2_ref.dtype)


def _peer_select(x1, wq_bf, q_g, keys_bf, n_heads, d_key, n_keys):
    tp, d_model = x1.shape
    tb = SEL_TB
    row_spec = pl.BlockSpec((n_heads, n_keys, tb), lambda i: (0, 0, i))
    tile_spec = pl.BlockSpec((n_heads, n_keys // 2, tb), lambda i: (0, 0, i))
    out_shape = [jax.ShapeDtypeStruct((n_heads, n_keys, tp), F32)] * 2 + [
        jax.ShapeDtypeStruct((n_heads, n_keys // 2, tp), jnp.uint32)] * 2
    return pl.pallas_call(
        functools.partial(_peer_sel_kernel, n_heads=n_heads, d_key=d_key, topk=PEER_TOPK),
        grid=(tp // tb,),
        in_specs=[pl.BlockSpec((tb, d_model), lambda i: (i, 0)),
                  pl.BlockSpec(wq_bf.shape, lambda i: (0, 0)),
                  pl.BlockSpec((1, d_key), lambda i: (0, 0)),
                  pl.BlockSpec(keys_bf.shape, lambda i: (0, 0, 0))],
        out_specs=[row_spec, row_spec, tile_spec, tile_spec],
        out_shape=out_shape,
        compiler_params=_params("arbitrary"),
        name="peer_select",
    )(x1, wq_bf, q_g, keys_bf)


def _peer_dense_kernel(x_ref, u_ref, vt_ref, n1_ref, e1_ref, r2_ref, e2_ref,
                       g_ref, b_ref, o_ref, xt_ref, xtb_ref, acc_ref, act_ref, wg_ref,
                       *, n_heads, n_keys, lane_tile, alpha):
    c = pl.program_id(1)
    n_chunks = pl.num_programs(1) - 1
    n_i = act_ref.shape[0] // n_keys
    tb = act_ref.shape[1]
    zero = jnp.zeros((n_keys, lane_tile), BF16)

    @pl.when(c == 0)
    def _():
        acc_ref[...] = jnp.zeros(acc_ref.shape, F32)
        wg_ref[1] = jnp.zeros(wg_ref.shape[1:], BF16)
        xt = x_ref[...].T
        xt_ref[...] = xt
        xtb_ref[...] = xt.astype(BF16)

    def row_tile(ref, h, il, cols):
        row = jnp.broadcast_to(ref[h, il:il + 1, cols], (BF16_SUBLANES, lane_tile)).astype(BF16)
        return jnp.tile(row, (n_keys // BF16_SUBLANES, 1))

    def pv_previous():
        acc_ref[...] += jnp.dot(vt_ref[...], wg_ref[(c + 1) % 2], preferred_element_type=F32)

    @pl.when(c < n_chunks)
    def _():
        pv_previous()
        act_ref[...] = jnp.dot(u_ref[...], xtb_ref[...], preferred_element_type=F32)
        cur = c % 2
        for tc in range(tb // lane_tile):
            cols = slice(tc * lane_tile, (tc + 1) * lane_tile)
            for il in range(n_i):
                w = zero
                for h in range(n_heads):
                    sel = pltpu.bitcast(r2_ref[h, :, cols], BF16) < row_tile(n1_ref, h, il, cols)
                    e2 = pltpu.bitcast(e2_ref[h, :, cols], BF16)
                    w = w + jnp.where(sel, e2, zero) * row_tile(e1_ref, h, il, cols)
                rows = slice(il * n_keys, (il + 1) * n_keys)
                wg_ref[cur, rows, cols] = w * _gelu(act_ref[rows, cols]).astype(BF16)

    @pl.when(c == n_chunks)
    def _():
        pv_previous()
        o_ref[...] = _layer_norm(alpha * xt_ref[...] + acc_ref[...], g_ref[...], b_ref[...], 0).T


def _peer_dense(x1, u_bf, vt_bf, n1, e1, r2, e2, ln_g, ln_b, alpha):
    tp, d_model = x1.shape
    n_heads, n_keys, _ = n1.shape
    n_exp = u_bf.shape[0]
    tb, ec = PEER_TB, PEER_EC
    n_i = ec // n_keys
    n_chunks = n_exp // ec
    tok = lambda t, c: (t, 0)
    cur = lambda c: jnp.minimum(c, n_chunks - 1)
    prev = lambda c: jnp.maximum(c - 1, 0)
    return pl.pallas_call(
        functools.partial(_peer_dense_kernel, n_heads=n_heads, n_keys=n_keys,
                          lane_tile=PEER_LANE_TILE, alpha=alpha),
        grid=(tp // tb, n_chunks + 1),
        in_specs=[pl.BlockSpec((tb, d_model), tok),
                  pl.BlockSpec((ec, d_model), lambda t, c: (cur(c), 0)),
                  pl.BlockSpec((d_model, ec), lambda t, c: (0, prev(c))),
                  pl.BlockSpec((n_heads, n_i, tb), lambda t, c: (0, cur(c), t)),
                  pl.BlockSpec((n_heads, n_i, tb), lambda t, c: (0, cur(c), t)),
                  pl.BlockSpec((n_heads, n_keys // 2, tb), lambda t, c: (0, 0, t)),
                  pl.BlockSpec((n_heads, n_keys // 2, tb), lambda t, c: (0, 0, t)),
                  pl.BlockSpec((d_model, 1), lambda t, c: (0, 0)),
                  pl.BlockSpec((d_model, 1), lambda t, c: (0, 0))],
        out_specs=pl.BlockSpec((tb, d_model), tok),
        out_shape=jax.ShapeDtypeStruct((tp, d_model), F32),
        scratch_shapes=[pltpu.VMEM((d_model, tb), F32), pltpu.VMEM((d_model, tb), BF16),
                        pltpu.VMEM((d_model, tb), F32), pltpu.VMEM((ec, tb), F32),
                        pltpu.VMEM((2, ec, tb), BF16)],
        compiler_params=_params("arbitrary", "arbitrary"),
        name="peer_dense",
    )(x1, u_bf, vt_bf, n1, e1, r2, e2, ln_g, ln_b)


def kernel(x_prompt, x_sample, cache_k, cache_v, state_ssm_re, state_ssm_im, page_table,
           w_in, lambda_q1, lambda_k1, lambda_q2, lambda_k2, attn_subln_g,
           ssm_lambda_re, ssm_lambda_im, ssm_log_dt, ssm_b_re, ssm_b_im, ssm_c_re, ssm_c_im,
           ssm_d, ssm_w_glu, ssm_b_glu, w_out, ln1_g, ln1_b,
           peer_w_q, peer_q_g, peer_sub_keys, peer_u, peer_v, ln2_g, ln2_b):
    depth = w_in.shape[0]
    batch, seq, d_model = x_prompt.shape
    dec_batch, dec_seq, _ = x_sample.shape
    _, n_phys, page, n_heads, _, dqk = cache_k.shape
    dv = cache_v.shape[-1]
    d_attn = n_heads * dv
    groups, p_dim = state_ssm_re.shape[2:]
    ch = ssm_b_re.shape[-1]
    peer_heads, _, n_keys, d_half = peer_sub_keys.shape[1:]
    d_key = 2 * d_half
    alpha = (2.0 * depth) ** 0.25
    scale = dqk ** -0.5

    n_prompt = batch * seq
    n_sample = dec_batch * dec_seq
    n_tok = n_prompt + n_sample
    tp = -(-n_tok // TOKEN_PAD) * TOKEN_PAD
    n_chunks = seq // SSM_CHUNK
    n_steps = int(math.log2(n_chunks))
    assert n_chunks == 1 << n_steps and seq % ATTN_TQ == 0 and n_prompt % TOKEN_PAD == 0

    x_all = jnp.concatenate(
        [x_prompt.reshape(n_prompt, d_model), x_sample.reshape(n_sample, d_model),
         jnp.zeros((tp - n_tok, d_model), x_prompt.dtype)], axis=0)
    slope_vals = 2.0 ** (-8.0 * jnp.arange(1, n_heads + 1, dtype=F32) / n_heads)
    slopes = jnp.broadcast_to(slope_vals[:, None, None], (n_heads, 1, 128))
    slopecol = jnp.repeat(slope_vals, 2 * dec_seq)[:, None]
    eye_hc = jnp.eye(2 * n_heads, dtype=F32)

    outs = {name: [] for name in ("kp", "vp", "hpr", "hpi", "ks", "vs", "hsr", "hsi")}
    for l in range(depth):
        lam_init = 0.8 - 0.6 * math.exp(-0.3 * l)
        lamv = jnp.stack([lambda_q1[l], lambda_k1[l], lambda_q2[l], lambda_k2[l]]).astype(F32)
        subln_g = attn_subln_g[l].astype(F32)[None, :]

        w_bf = w_in[l].astype(BF16)
        kt_prompt, k_last, v_prompt, v_last, u_all, qkv_bf = _project(
            x_all, w_bf, w_bf[:, d_attn:2 * d_attn].T, d_attn, n_heads, scale, batch, seq)

        a_prompt = _attn_prompt(qkv_bf, lamv, subln_g, slopes, batch, seq, n_heads, dv, dqk,
                                lam_init)

        qkv_s = qkv_bf[n_prompt:n_tok].reshape(dec_batch, dec_seq, 3 * d_attn)
        q_s = qkv_s[..., :d_attn].reshape(dec_batch, dec_seq, 2 * n_heads, dqk)
        qbd = (q_s.transpose(0, 2, 1, 3)[:, :, :, None, :].astype(F32)
               * eye_hc[None, :, None, :, None]).astype(BF16)
        qbd = qbd.reshape(dec_batch, 2 * n_heads * dec_seq, d_attn)
        k_new = jnp.pad(qkv_s[..., d_attn:2 * d_attn].transpose(0, 2, 1),
                        ((0, 0), (0, 0), (0, page - dec_seq)))
        v_new = jnp.pad(qkv_s[..., 2 * d_attn:], ((0, 0), (0, page - dec_seq), (0, 0)))
        kt2d = jnp.transpose(cache_k[l], (0, 2, 3, 4, 1)).reshape(n_phys * d_attn, page)
        a_s = _attn_sample(page_table, lamv, subln_g, slopecol, qbd, kt2d,
                           cache_v[l].reshape(n_phys * page * n_heads, dv),
                           k_new, v_new, n_heads, dec_seq, lam_init)
        a_sample = a_s.reshape(dec_batch, n_heads, dec_seq, dv).transpose(0, 2, 1, 3)
        a_all = jnp.concatenate(
            [a_prompt, a_sample.reshape(n_sample, d_attn),
             jnp.zeros((tp - n_tok, d_attn), BF16)], axis=0)

        ssm_p = (ssm_lambda_re[l], ssm_lambda_im[l], ssm_log_dt[l], ssm_b_re[l], ssm_b_im[l],
                 ssm_c_re[l], ssm_c_im[l], ssm_d[l])
        tm, sm, cm, dsk, coef = _ssm_tables(*ssm_p, SSM_CHUNK, n_steps)
        u_r = (u_all[:n_prompt].reshape(batch, n_chunks, SSM_CHUNK, groups, ch)
               .transpose(0, 3, 1, 2, 4).reshape(batch, groups, n_chunks, SSM_CHUNK * ch))
        y_r, h_p = _ssm_prompt(u_r, tm, sm, cm, dsk, coef)
        yg_prompt = (y_r.reshape(batch, groups, n_chunks, SSM_CHUNK, ch)
                     .transpose(0, 2, 3, 1, 4).reshape(n_prompt, groups * ch))

        tm_s, sm_s, cm_s, dsk_s, coef_s = _ssm_tables(*ssm_p, dec_seq, 1)
        us_r = (u_all[n_prompt:n_tok].reshape(dec_batch, dec_seq, groups, ch)
                .transpose(2, 0, 1, 3).reshape(groups, dec_batch, dec_seq * ch))
        h0r = state_ssm_re[l].astype(F32).transpose(1, 0, 2)
        h0i = state_ssm_im[l].astype(F32).transpose(1, 0, 2)
        ys_r, h_s = _ssm_sample(us_r, jnp.concatenate([h0r, h0i], -1),
                                jnp.concatenate([h0i, h0r], -1),
                                tm_s, sm_s[..., :2 * p_dim], cm_s, dsk_s, coef_s)
        yg_sample = (ys_r.reshape(groups, dec_batch, dec_seq, ch)
                     .transpose(1, 2, 0, 3).reshape(n_sample, groups * ch))
        yg_all = jnp.concatenate(
            [yg_prompt, yg_sample, jnp.zeros((tp - n_tok, groups * ch), F32)], axis=0)

        x1 = _out_proj(x_all, a_all, yg_all, ssm_w_glu[l].astype(BF16),
                       ssm_b_glu[l].astype(F32)[None, :], w_out[l].astype(BF16),
                       ln1_g[l].astype(F32)[None, :], ln1_b[l].astype(F32)[None, :], alpha)

        keys_bf = peer_sub_keys[l].reshape(2 * peer_heads, n_keys, d_half).astype(BF16)
        n1, e1, r2, e2 = _peer_select(x1, peer_w_q[l].astype(BF16),
                                      peer_q_g[l].astype(F32)[None, :], keys_bf,
                                      peer_heads, d_key, n_keys)
        x_all = _peer_dense(x1, peer_u[l].astype(BF16),
                            peer_v[l].T.astype(BF16), n1, e1, r2, e2,
                            ln2_g[l].astype(F32)[:, None], ln2_b[l].astype(F32)[:, None], alpha)

        outs["kp"].append(kt_prompt.reshape(batch, n_heads, 2, dqk, seq).transpose(0, 4, 1, 2, 3))
        outs["vp"].append(v_prompt.reshape(batch, seq, n_heads, dv))
        outs["hpr"].append(h_p[:, :, 0, :p_dim])
        outs["hpi"].append(h_p[:, :, 0, p_dim:])
        outs["ks"].append(k_last[:n_sample].reshape(dec_batch, dec_seq, n_heads, 2, dqk))
        outs["vs"].append(v_last[:n_sample * n_heads].reshape(dec_batch, dec_seq, n_heads, dv))
        outs["hsr"].append(h_s[:, :, :p_dim].transpose(1, 0, 2))
        outs["hsi"].append(h_s[:, :, p_dim:].transpose(1, 0, 2))

    y_prompt = x_all[:n_prompt].reshape(batch, seq, d_model)
    y_sample = x_all[n_prompt:n_tok].reshape(dec_batch, dec_seq, d_model)
    st = lambda name: jnp.stack(outs[name])
    return (y_prompt, y_sample, st("kp"), st("vp"), st("hpr"), st("hpi"),
            st("ks"), st("vs"), st("hsr"), st("hsi"))
```

```python
import functools
import math

import jax
import jax.numpy as jnp
from jax import lax
from jax.experimental import pallas as pl
from jax.experimental.pallas import tpu as pltpu

F32 = jnp.float32
BF16 = jnp.bfloat16

EPS = 1e-5
MASK_VALUE = -1e30
PEER_TOPK = 16
NEG_INF = float("-inf")

TOKEN_PAD = 512
PROJ_TM = 512
ATTN_TQ = 1024
ATTN_TK = 1024
ATTN_RS = 512
SAMPLE_PAGES = 16
SSM_CHUNK = 16
SEL_TB = 256
PEER_TB = 512
PEER_EC = 1024
BF16_SUBLANES = 16
PEER_LANE_TILE = 128
VMEM_LIMIT = 56 * 1024 * 1024

_NT = (((1,), (1,)), ((), ()))
_HI = lax.Precision.HIGHEST


def _params(*sem):
    return pltpu.CompilerParams(dimension_semantics=sem, vmem_limit_bytes=VMEM_LIMIT)


def _gelu(x):
    return 0.5 * x * (1.0 + lax.erf(x * (2.0 ** -0.5)))


def _block_rows(i, n_prompt_blocks, prompt_ref, tail_ref):
    return jnp.where(i < n_prompt_blocks, prompt_ref[...], tail_ref[...])


def _proj_kernel(xp_ref, xt_ref, w_ref, wkt_ref, kt_ref, ks_ref, vp_ref, vs_ref, u_ref, qkv_ref,
                 *, d_attn, n_heads, scale, n_prompt_blocks):
    i = pl.program_id(0)
    x = _block_rows(i, n_prompt_blocks, xp_ref, xt_ref).astype(BF16)
    z = jnp.dot(x, w_ref[...], preferred_element_type=F32)
    k = z[:, d_attn:2 * d_attn]
    v = z[:, 2 * d_attn:3 * d_attn]
    u_ref[...] = z[:, 3 * d_attn:]
    qkv_ref[:, :d_attn] = (z[:, :d_attn] * scale).astype(BF16)
    qkv_ref[:, d_attn:2 * d_attn] = k.astype(BF16)
    qkv_ref[:, 2 * d_attn:] = v.astype(BF16)
    tm = x.shape[0]
    dv = d_attn // n_heads

    def store_v(ref):
        for h in range(n_heads):
            ref[pl.ds(h, tm, stride=n_heads), :] = v[:, h * dv:(h + 1) * dv]

    @pl.when(i < n_prompt_blocks)
    def _():
        kt_ref[...] = lax.dot_general(wkt_ref[...], x, _NT, preferred_element_type=F32)
        store_v(vp_ref)

    @pl.when(i == pl.num_programs(0) - 1)
    def _():
        ks_ref[...] = k
        store_v(vs_ref)


def _project(x_prompt, x_tail, w_bf, wkt_bf, d_attn, n_heads, scale, batch, seq):
    d_model = x_prompt.shape[1]
    tp = x_prompt.shape[0] + x_tail.shape[0]
    n_out = w_bf.shape[1]
    d_u = n_out - 3 * d_attn
    dv = d_attn // n_heads
    tm = PROJ_TM
    per_seq = seq // tm
    n_prompt_blocks = batch * per_seq
    assert tp // tm == n_prompt_blocks + 1
    last = n_prompt_blocks - 1
    row = lambda i: (i, 0)
    kt_map = lambda i: (jnp.minimum(i, last) // per_seq, 0, jnp.minimum(i, last) % per_seq)
    return pl.pallas_call(
        functools.partial(_proj_kernel, d_attn=d_attn, n_heads=n_heads, scale=scale,
                          n_prompt_blocks=n_prompt_blocks),
        grid=(tp // tm,),
        in_specs=[pl.BlockSpec((tm, d_model), lambda i: (jnp.minimum(i, last), 0)),
                  pl.BlockSpec((tm, d_model), lambda i: (0, 0)),
                  pl.BlockSpec((d_model, n_out), lambda i: (0, 0)),
                  pl.BlockSpec((d_attn, d_model), lambda i: (0, 0))],
        out_specs=[pl.BlockSpec((None, d_attn, tm), kt_map),
                   pl.BlockSpec((tm, d_attn), lambda i: (0, 0)),
                   pl.BlockSpec((tm * n_heads, dv), lambda i: (jnp.minimum(i, last), 0)),
                   pl.BlockSpec((tm * n_heads, dv), lambda i: (0, 0)),
                   pl.BlockSpec((tm, d_u), row),
                   pl.BlockSpec((tm, 3 * d_attn), row)],
        out_shape=[jax.ShapeDtypeStruct((batch, d_attn, seq), F32),
                   jax.ShapeDtypeStruct((tm, d_attn), F32),
                   jax.ShapeDtypeStruct((batch * seq * n_heads, dv), F32),
                   jax.ShapeDtypeStruct((tm * n_heads, dv), F32),
                   jax.ShapeDtypeStruct((tp, d_u), F32),
                   jax.ShapeDtypeStruct((tp, 3 * d_attn), BF16)],
        compiler_params=_params("arbitrary"),
        name="proj",
    )(x_prompt, x_tail, w_bf, wkt_bf)


def _lam_value(lamv_ref, lam_init):
    lv = lamv_ref[...]
    e1 = jnp.exp(jnp.sum(lv[0:1] * lv[1:2], axis=1, keepdims=True))
    e2 = jnp.exp(jnp.sum(lv[2:3] * lv[3:4], axis=1, keepdims=True))
    return e1 - e2 + lam_init


def _subln(o, g, lam_init):
    ms = jnp.mean(o * o, axis=-1, keepdims=True)
    return o * lax.rsqrt(ms + EPS) * g * (1.0 - lam_init)


def _attn_prompt_kernel(qi_ref, kj_ref, lamv_ref, g_ref, slope_ref, q_ref, k_ref, v_ref, o_ref,
                        m_ref, l_ref, acc_ref, *, tq, tk, rs, dqk, lam_init):
    t = pl.program_id(2)
    i = qi_ref[t]
    j = kj_ref[t]
    lanes = m_ref.shape[-1]
    ratio = tq // tk
    off = j - ratio * i

    @pl.when(j == 0)
    def _():
        m_ref[...] = jnp.full(m_ref.shape, MASK_VALUE, F32)
        l_ref[...] = jnp.zeros(l_ref.shape, F32)
        acc_ref[...] = jnp.zeros(acc_ref.shape, F32)

    n_sub = tq // rs

    def step(shift):
        kcol = lax.broadcasted_iota(jnp.int32, (1, tk), 1) + (j * tk - i * tq)
        colbias = slope_ref[...][:, :1] * kcol.astype(F32)

        def n_keys(r):
            return tk if shift is None else max(0, min(tk, (r + 1) * rs - shift))

        def masked(r):
            return shift is not None and 0 < (r + 1) * rs - shift <= tk

        def scores(r):
            q = q_ref[r * rs:(r + 1) * rs, :]
            lane = lax.broadcasted_iota(jnp.int32, q.shape, 1)
            zero = jnp.zeros_like(q)
            q2 = jnp.concatenate([jnp.where(lane < dqk, q, zero), jnp.where(lane < dqk, zero, q)], axis=0)
            return lax.dot_general(q2, k_ref[:n_keys(r), :], _NT, preferred_element_type=F32)

        blocks = [r for r in range(n_sub) if n_keys(r) > 0]
        ahead = 2
        pending = {r: scores(r) for r in blocks[:ahead]}
        for n, r in enumerate(blocks):
            if n + ahead < len(blocks):
                pending[blocks[n + ahead]] = scores(blocks[n + ahead])
            nk = n_keys(r)
            rows = slice(2 * r * rs, 2 * (r + 1) * rs)
            s = pending.pop(r) + colbias[:, :nk]
            if masked(r):
                row = lax.broadcasted_iota(jnp.int32, (2 * rs, nk), 0) & (rs - 1)
                col = lax.broadcasted_iota(jnp.int32, (2 * rs, nk), 1)
                s = jnp.where(row + (r * rs - shift) >= col, s, MASK_VALUE)
            m_prev = m_ref[rows, :]
            m_new = jnp.maximum(m_prev, jnp.max(s, axis=1, keepdims=True))
            alpha = jnp.exp(m_prev - m_new)
            p = jnp.exp(s - jnp.tile(m_new, (1, nk // lanes)))
            l_ref[rows, :] = alpha * l_ref[rows, :] + jnp.sum(p, axis=1, keepdims=True)
            acc_ref[rows, :] = alpha * acc_ref[rows, :] + jnp.dot(
                p.astype(BF16), v_ref[:nk, :], preferred_element_type=F32)
            m_ref[rows, :] = m_new

    @pl.when(off < 0)
    def _():
        step(None)

    for d in range(ratio):
        pl.when(off == d)(functools.partial(step, d * tk))

    @pl.when(off == ratio - 1)
    def _():
        lam = _lam_value(lamv_ref, lam_init)
        for r in range(n_sub):
            r1 = slice(2 * r * rs, (2 * r + 1) * rs)
            r2 = slice((2 * r + 1) * rs, 2 * (r + 1) * rs)
            o = acc_ref[r1, :] / l_ref[r1, :] - lam * (acc_ref[r2, :] / l_ref[r2, :])
            o_ref[r * rs:(r + 1) * rs, :] = _subln(o, g_ref[...], lam_init).astype(o_ref.dtype)


def _attn_prompt(qkv_bf, lamv, subln_g, slopes, batch, seq, n_heads, dv, dqk, lam_init):
    tq, tk = ATTN_TQ, ATTN_TK
    nq, nk = seq // tq, seq // tk
    ratio = tq // tk
    d_attn = n_heads * dv
    pairs = [(i, j) for i in range(nq) for j in range(ratio * (i + 1))]
    qi = jnp.asarray([p[0] for p in pairs], jnp.int32)
    kj = jnp.asarray([p[1] for p in pairs], jnp.int32)
    grid_spec = pltpu.PrefetchScalarGridSpec(
        num_scalar_prefetch=2,
        grid=(batch, n_heads, len(pairs)),
        in_specs=[pl.BlockSpec(lamv.shape, lambda b, h, t, qi, kj: (0, 0)),
                  pl.BlockSpec((1, dv), lambda b, h, t, qi, kj: (0, 0)),
                  pl.BlockSpec((None, 1, 128), lambda b, h, t, qi, kj: (h, 0, 0)),
                  pl.BlockSpec((tq, dv), lambda b, h, t, qi, kj: (b * nq + qi[t], h)),
                  pl.BlockSpec((tk, dv), lambda b, h, t, qi, kj: (b * nk + kj[t], n_heads + h)),
                  pl.BlockSpec((tk, dv), lambda b, h, t, qi, kj: (b * nk + kj[t], 2 * n_heads + h))],
        out_specs=pl.BlockSpec((tq, dv), lambda b, h, t, qi, kj: (b * nq + qi[t], h)),
        scratch_shapes=[pltpu.VMEM((2 * tq, dv), F32), pltpu.VMEM((2 * tq, dv), F32),
                        pltpu.VMEM((2 * tq, dv), F32)],
    )
    return pl.pallas_call(
        functools.partial(_attn_prompt_kernel, tq=tq, tk=tk, rs=ATTN_RS, dqk=dqk, lam_init=lam_init),
        grid_spec=grid_spec,
        out_shape=jax.ShapeDtypeStruct((batch * seq, d_attn), BF16),
        compiler_params=_params("arbitrary", "arbitrary", "arbitrary"),
        name="attn_prompt",
    )(qi, kj, lamv, subln_g, slopes, qkv_bf, qkv_bf, qkv_bf)


def _attn_sample_kernel(pt_ref, lamv_ref, g_ref, slopecol_ref, qbd_ref, kn_ref, vn_ref, *rest,
                        pp, page, past_len, n_heads, n_new, lam_init):
    del pt_ref
    k_refs, v_refs = rest[:pp], rest[pp:2 * pp]
    o_ref, m_ref, l_ref, acc_ref = rest[2 * pp:]
    step = pl.program_id(1)
    rows_h = 2 * n_new
    dv = m_ref.shape[1]

    @pl.when(step == 0)
    def _():
        m_ref[...] = jnp.full(m_ref.shape, MASK_VALUE, F32)
        l_ref[...] = jnp.zeros(l_ref.shape, F32)
        acc_ref[...] = jnp.zeros(acc_ref.shape, F32)

    def update(s, v):
        m_prev = m_ref[...]
        m_new = jnp.maximum(m_prev, jnp.max(s, axis=1, keepdims=True))
        alpha = jnp.exp(m_prev - m_new)
        p = jnp.exp(s - jnp.tile(m_new, (1, s.shape[1] // dv)))
        l_ref[...] = alpha * l_ref[...] + jnp.sum(p, axis=1, keepdims=True)
        acc_ref[...] = jnp.tile(alpha, (1, n_heads)) * acc_ref[...] + jnp.dot(
            p.astype(BF16), v, preferred_element_type=F32)
        m_ref[...] = m_new

    def page_values(n):
        return jnp.concatenate(
            [v_refs[n][pl.ds(h, page, stride=n_heads), :] for h in range(n_heads)], axis=1).astype(BF16)

    slope = slopecol_ref[...]
    kt = jnp.concatenate([k_refs[n][...].astype(BF16) for n in range(pp)], axis=1)
    kpos = lax.broadcasted_iota(jnp.int32, (1, pp * page), 1) + (step * (pp * page) - past_len)
    s = jnp.dot(qbd_ref[...], kt, preferred_element_type=F32) + slope * kpos.astype(F32)
    update(s, jnp.concatenate([page_values(n) for n in range(pp)], axis=0))

    @pl.when(step == pl.num_programs(1) - 1)
    def _():
        s = jnp.dot(qbd_ref[...], kn_ref[...], preferred_element_type=F32)
        col = lax.broadcasted_iota(jnp.int32, s.shape, 1)
        qi = lax.broadcasted_iota(jnp.int32, s.shape, 0) & (n_new - 1)
        s = s + slope * col.astype(F32)
        update(jnp.where(col <= qi, s, MASK_VALUE), vn_ref[...])
        lam = _lam_value(lamv_ref, lam_init)
        for h in range(n_heads):
            r = slice(h * rows_h, (h + 1) * rows_h)
            a = acc_ref[r, h * dv:(h + 1) * dv] / l_ref[r, :]
            o = a[:n_new] - lam * a[n_new:]
            o_ref[h * n_new:(h + 1) * n_new, :] = _subln(o, g_ref[...], lam_init).astype(o_ref.dtype)


def _attn_sample(page_table, lamv, subln_g, slopecol, qbd, kt2d, v2d, k_new, v_new,
                 n_heads, n_new, lam_init):
    dec_batch, n_pages = page_table.shape
    page, dv = kt2d.shape[1], v2d.shape[1]
    n_rows, d_attn = qbd.shape[1:]
    pp = SAMPLE_PAGES
    assert n_pages % pp == 0
    pt_flat = page_table.reshape(-1)
    fixed = lambda b, s, pt: (0, 0)
    seq3 = lambda b, s, pt: (b, 0, 0)

    def page_map(n):
        return lambda b, s, pt: (pt[b * n_pages + s * pp + n], 0)

    kernel = functools.partial(
        _attn_sample_kernel, pp=pp, page=page, past_len=n_pages * page,
        n_heads=n_heads, n_new=n_new, lam_init=lam_init)
    grid_spec = pltpu.PrefetchScalarGridSpec(
        num_scalar_prefetch=1,
        grid=(dec_batch, n_pages // pp),
        in_specs=[pl.BlockSpec(lamv.shape, fixed),
                  pl.BlockSpec((1, dv), fixed),
                  pl.BlockSpec((n_rows, 1), fixed),
                  pl.BlockSpec((None, n_rows, d_attn), seq3),
                  pl.BlockSpec((None,) + k_new.shape[1:], seq3),
                  pl.BlockSpec((None,) + v_new.shape[1:], seq3)]
        + [pl.BlockSpec((d_attn, page), page_map(n)) for n in range(pp)]
        + [pl.BlockSpec((page * n_heads, dv), page_map(n)) for n in range(pp)],
        out_specs=pl.BlockSpec((None, n_heads * n_new, dv), seq3),
        scratch_shapes=[pltpu.VMEM((n_rows, dv), F32), pltpu.VMEM((n_rows, dv), F32),
                        pltpu.VMEM((n_rows, n_heads * dv), F32)],
    )
    return pl.pallas_call(
        kernel,
        grid_spec=grid_spec,
        out_shape=jax.ShapeDtypeStruct((dec_batch, n_heads * n_new, dv), BF16),
        compiler_params=_params("arbitrary", "arbitrary"),
        name="attn_sample",
    )(pt_flat, lamv, subln_g, slopecol, qbd, k_new, v_new, *([kt2d] * pp), *([v2d] * pp))


def _ssm_prompt_kernel(u_ref, tm_ref, sm_ref, cm_ref, d_ref, coef_ref, y_ref, h_ref,
                       *, n_chunks, n_steps):
    u = u_ref[...]
    x = jnp.dot(u, sm_ref[...], precision=_HI, preferred_element_type=F32)
    half = x.shape[1] // 2
    x0, x1 = x[:, :half], x[:, half:]
    row = lax.broadcasted_iota(jnp.int32, x0.shape, 0)

    def shifted(a, d):
        return jnp.where(row >= d, pltpu.roll(a, d, 0), 0.0)

    y0, y1 = shifted(x0, 1), shifted(x1, 1)
    coef = coef_ref[...]
    for s in range(n_steps):
        d = 1 << s
        p = coef[2 * s:2 * s + 1]
        q = coef[2 * s + 1:2 * s + 2]
        s0, s1 = shifted(y0, d), shifted(y1, d)
        y0, y1 = y0 + p * s0 + q * s1, y1 + p * s1 - q * s0
    last = n_chunks - 1
    h_ref[...] = (coef[0:1] * y0[last:last + 1] + coef[1:2] * y1[last:last + 1]
                  + x0[last:last + 1])
    y = (jnp.dot(u, tm_ref[...], precision=_HI, preferred_element_type=F32)
         + jnp.dot(y0, cm_ref[...], precision=_HI, preferred_element_type=F32)
         + u * d_ref[...])
    y_ref[...] = _gelu(y)


def _ssm_prompt(u_r, tm, sm, cm, dsk, coef):
    batch, groups, n_chunks, width = u_r.shape
    n_steps = coef.shape[1] // 2
    state = cm.shape[1]
    return pl.pallas_call(
        functools.partial(_ssm_prompt_kernel, n_chunks=n_chunks, n_steps=n_steps),
        grid=(batch, groups),
        in_specs=[pl.BlockSpec((None, None, n_chunks, width), lambda b, g: (b, g, 0, 0)),
                  pl.BlockSpec((None, width, width), lambda b, g: (g, 0, 0)),
                  pl.BlockSpec((None, width, 2 * state), lambda b, g: (g, 0, 0)),
                  pl.BlockSpec((None, state, width), lambda b, g: (g, 0, 0)),
                  pl.BlockSpec((None, 1, width), lambda b, g: (g, 0, 0)),
                  pl.BlockSpec((None, 2 * n_steps, state), lambda b, g: (g, 0, 0))],
        out_specs=[pl.BlockSpec((None, None, n_chunks, width), lambda b, g: (b, g, 0, 0)),
                   pl.BlockSpec((None, None, 1, state), lambda b, g: (b, g, 0, 0))],
        out_shape=[jax.ShapeDtypeStruct((batch, groups, n_chunks, width), F32),
                   jax.ShapeDtypeStruct((batch, groups, 1, state), F32)],
        compiler_params=_params("arbitrary", "arbitrary"),
        name="ssm_prompt",
    )(u_r, tm, sm, cm, dsk, coef)


def _ssm_sample_kernel(u_ref, h0_ref, h0s_ref, tm_ref, sm_ref, cm_ref, d_ref, coef_ref,
                       y_ref, h_ref):
    u = u_ref[...]
    h0 = h0_ref[...]
    coef = coef_ref[...]
    x = jnp.dot(u, sm_ref[...], precision=_HI, preferred_element_type=F32)
    h_ref[...] = coef[0:1] * h0 + coef[1:2] * h0s_ref[...] + x
    y = (jnp.dot(u, tm_ref[...], precision=_HI, preferred_element_type=F32)
         + jnp.dot(h0, cm_ref[...], precision=_HI, preferred_element_type=F32)
         + u * d_ref[...])
    y_ref[...] = _gelu(y)


def _ssm_sample(u_r, h0, h0s, tm, sm, cm, dsk, coef):
    groups, seqs, width = u_r.shape
    state = h0.shape[2]
    return pl.pallas_call(
        _ssm_sample_kernel,
        grid=(groups,),
        in_specs=[pl.BlockSpec((None, seqs, width), lambda g: (g, 0, 0)),
                  pl.BlockSpec((None, seqs, state), lambda g: (g, 0, 0)),
                  pl.BlockSpec((None, seqs, state), lambda g: (g, 0, 0)),
                  pl.BlockSpec((None, width, width), lambda g: (g, 0, 0)),
                  pl.BlockSpec((None, width, state), lambda g: (g, 0, 0)),
                  pl.BlockSpec((None, state, width), lambda g: (g, 0, 0)),
                  pl.BlockSpec((None, 1, width), lambda g: (g, 0, 0)),
                  pl.BlockSpec((None, 2, state), lambda g: (g, 0, 0))],
        out_specs=[pl.BlockSpec((None, seqs, width), lambda g: (g, 0, 0)),
                   pl.BlockSpec((None, seqs, state), lambda g: (g, 0, 0))],
        out_shape=[jax.ShapeDtypeStruct((groups, seqs, width), F32),
                   jax.ShapeDtypeStruct((groups, seqs, state), F32)],
        compiler_params=_params("arbitrary"),
        name="ssm_sample",
    )(u_r, h0, h0s, tm, sm, cm, dsk, coef)


def _ssm_tables(lam_re, lam_im, log_dt, b_re, b_im, c_re, c_im, d_skip, chunk, n_steps):
    groups, p_dim, ch = b_re.shape
    lr, li = lam_re.astype(F32), lam_im.astype(F32)
    dt = jnp.exp(log_dt.astype(F32))[:, None]
    mag = jnp.exp(lr * dt)
    a_re, a_im = mag * jnp.cos(li * dt), mag * jnp.sin(li * dt)
    den = lr * lr + li * li
    n_re, n_im = a_re - 1.0, a_im
    g_re = (n_re * lr + n_im * li) / den
    g_im = (n_im * lr - n_re * li) / den
    br, bi = b_re.astype(F32), b_im.astype(F32)
    bb_re = g_re[..., None] * br - g_im[..., None] * bi
    bb_im = g_re[..., None] * bi + g_im[..., None] * br

    def power(n):
        n = n.astype(F32)[:, None, None]
        m = jnp.exp(lr * dt * n)
        return m * jnp.cos(li * dt * n), m * jnp.sin(li * dt * n)

    taus = jnp.arange(chunk + 1)
    pw_re, pw_im = power(taus)
    ab_re = pw_re[:chunk, :, :, None] * bb_re - pw_im[:chunk, :, :, None] * bb_im
    ab_im = pw_re[:chunk, :, :, None] * bb_im + pw_im[:chunk, :, :, None] * bb_re
    cr, ci = c_re.astype(F32), c_im.astype(F32)
    kern = (jnp.einsum('gcp,tgpd->gtcd', cr, ab_re, precision=_HI)
            - jnp.einsum('gcp,tgpd->gtcd', ci, ab_im, precision=_HI))
    tm = jnp.stack([jnp.pad(kern[:, :chunk - s], ((0, 0), (s, 0), (0, 0), (0, 0)))
                    for s in range(chunk)], axis=1)
    tm = tm.transpose(0, 1, 4, 2, 3).reshape(groups, chunk * ch, chunk * ch)
    sm_re = ab_re[::-1].transpose(1, 0, 3, 2).reshape(groups, chunk * ch, p_dim)
    sm_im = ab_im[::-1].transpose(1, 0, 3, 2).reshape(groups, chunk * ch, p_dim)
    sm = jnp.concatenate([sm_re, sm_im, sm_im, sm_re], axis=-1)
    ca_re = cr[:, None] * pw_re[1:].transpose(1, 0, 2)[:, :, None, :] \
        - ci[:, None] * pw_im[1:].transpose(1, 0, 2)[:, :, None, :]
    ca_im = cr[:, None] * pw_im[1:].transpose(1, 0, 2)[:, :, None, :] \
        + ci[:, None] * pw_re[1:].transpose(1, 0, 2)[:, :, None, :]
    cm = jnp.concatenate([ca_re, -ca_im], axis=-1)
    cm = cm.transpose(0, 3, 1, 2).reshape(groups, 2 * p_dim, chunk * ch)
    dsk = jnp.tile(d_skip.astype(F32), (1, chunk)).reshape(groups, 1, chunk * ch)
    steps = chunk * (2 ** jnp.arange(n_steps))
    sp_re, sp_im = power(steps)
    p_rows = jnp.concatenate([sp_re, sp_re], axis=-1)
    q_rows = jnp.concatenate([-sp_im, sp_im], axis=-1)
    coef = jnp.stack([p_rows, q_rows], axis=1).reshape(2 * n_steps, groups, 2 * p_dim)
    return tm, sm, cm, dsk, coef.transpose(1, 0, 2)


def _layer_norm(y, g, b, axis):
    mu = jnp.mean(y, axis=axis, keepdims=True)
    var = jnp.mean(jnp.square(y - mu), axis=axis, keepdims=True)
    return (y - mu) * lax.rsqrt(var + EPS) * g + b


def _out_kernel(xp_ref, xt_ref, ap_ref, at_ref, ygp_ref, ygt_ref, wglu_ref, bglu_ref, wo_ref,
                g_ref, b_ref, x1_ref, *, d_attn, alpha, n_prompt_blocks):
    i = pl.program_id(0)
    x = _block_rows(i, n_prompt_blocks, xp_ref, xt_ref)
    a = _block_rows(i, n_prompt_blocks, ap_ref, at_ref)
    yg = _block_rows(i, n_prompt_blocks, ygp_ref, ygt_ref)
    gate = jax.nn.sigmoid(
        jnp.dot(yg.astype(BF16), wglu_ref[...], preferred_element_type=F32) + bglu_ref[...])
    s = (yg * gate).astype(BF16)
    h = (jnp.dot(a, wo_ref[:d_attn, :], preferred_element_type=F32)
         + jnp.dot(s, wo_ref[d_attn:, :], preferred_element_type=F32))
    x1_ref[...] = _layer_norm(alpha * x + h, g_ref[...], b_ref[...], -1)


def _out_proj(x_rows, a_rows, yg_rows, wglu_bf, bglu, wo_bf, ln_g, ln_b, alpha):
    d_model = x_rows[0].shape[1]
    d_attn = a_rows[0].shape[1]
    d_ssm = yg_rows[0].shape[1]
    tm = PROJ_TM
    n_prompt_blocks = x_rows[0].shape[0] // tm
    last = n_prompt_blocks - 1
    prompt = lambda i: (jnp.minimum(i, last), 0)
    fixed = lambda i: (0, 0)
    pair = lambda d: [pl.BlockSpec((tm, d), prompt), pl.BlockSpec((tm, d), fixed)]
    return pl.pallas_call(
        functools.partial(_out_kernel, d_attn=d_attn, alpha=alpha, n_prompt_blocks=n_prompt_blocks),
        grid=(n_prompt_blocks + 1,),
        in_specs=pair(d_model) + pair(d_attn) + pair(d_ssm) + [
            pl.BlockSpec((d_ssm, d_ssm), fixed),
            pl.BlockSpec((1, d_ssm), fixed), pl.BlockSpec((d_model, d_model), fixed),
            pl.BlockSpec((1, d_model), fixed), pl.BlockSpec((1, d_model), fixed)],
        out_specs=pl.BlockSpec((tm, d_model), lambda i: (i, 0)),
        out_shape=jax.ShapeDtypeStruct(((n_prompt_blocks + 1) * tm, d_model), F32),
        compiler_params=_params("arbitrary"),
        name="out_proj",
    )(*x_rows, *a_rows, *yg_rows, wglu_bf, bglu, wo_bf, ln_g, ln_b)


def _top_k_rows(s, k):
    n = s.shape[0]
    iota = lax.broadcasted_iota(jnp.int32, s.shape, 0).astype(F32)
    rank = jnp.full(s.shape, float(k), F32)
    vals, idxs = [], []
    for r in range(k):
        mx = jnp.max(s, axis=0, keepdims=True)
        ix = jnp.min(jnp.where(s == mx, iota, float(n)), axis=0, keepdims=True)
        hit = iota == ix
        rank = jnp.where(hit, float(r), rank)
        s = jnp.where(hit, NEG_INF, s)
        vals.append(mx)
        idxs.append(ix)
    return vals, idxs, rank


def _staircase(k):
    return [(a, b) for a in range(k) for b in range(k) if (a + 1) * (b + 1) <= k]


def _peer_sel_kernel(x1_ref, wq_ref, qg_ref, keys_ref, n1_ref, e1_ref, r2_ref, e2_ref,
                     *, n_heads, d_key, topk):
    qp = jnp.dot(x1_ref[...].astype(BF16), wq_ref[...], preferred_element_type=F32)
    d_half = d_key // 2
    pairs = _staircase(topk)
    n_pairs = len(pairs)
    n_rows = -(-n_pairs // 8) * 8
    tb = qp.shape[0]
    for h in range(n_heads):
        qh = qp[:, h * d_key:(h + 1) * d_key]
        qn = qh * lax.rsqrt(jnp.mean(qh * qh, axis=-1, keepdims=True) + EPS) * qg_ref[...]
        qb = qn.astype(BF16)
        s1 = lax.dot_general(keys_ref[2 * h], qb[:, :d_half], _NT, preferred_element_type=F32)
        s2 = lax.dot_general(keys_ref[2 * h + 1], qb[:, d_half:], _NT, preferred_element_type=F32)
        t1, i1, _ = _top_k_rows(s1, topk)
        t2, _, r2 = _top_k_rows(s2, topk)
        cand = jnp.concatenate(
            [t1[a] + t2[b] for a, b in pairs]
            + [jnp.full((n_rows - n_pairs, tb), NEG_INF, F32)], axis=0)
        iota = lax.broadcasted_iota(jnp.int32, cand.shape, 0).astype(F32)
        work = cand
        z = jnp.zeros((1, tb), F32)
        m = t1[0] + t2[0]
        for _ in range(topk):
            mx = jnp.max(work, axis=0, keepdims=True)
            ix = jnp.min(jnp.where(work == mx, iota, float(n_rows)), axis=0, keepdims=True)
            work = jnp.where(iota == ix, NEG_INF, work)
            z = z + jnp.exp(mx - m)
        chosen = jnp.where((work == NEG_INF) & (iota < float(n_pairs)), 1.0, 0.0)
        key_iota = lax.broadcasted_iota(jnp.int32, s1.shape, 0).astype(F32)
        n1 = jnp.zeros(s1.shape, F32)
        off = 0
        for a in range(topk):
            cnt = sum(1 for pa, _ in pairs if pa == a)
            n_a = jnp.sum(chosen[off:off + cnt], axis=0, keepdims=True)
            off += cnt
            n1 = jnp.where(key_iota == i1[a], n_a, n1)
        n1_ref[h] = n1.astype(n1_ref.dtype)
        e1_ref[h] = jnp.exp(s1 - t1[0]).astype(e1_ref.dtype)
        r2_ref[h] = pltpu.bitcast(r2.astype(BF16), r2_ref.dtype)
        e2_ref[h] = pltpu.bitcast((jnp.exp(s2 - t2[0]) / z).astype(BF16), e2_ref.dtype)


def _peer_select(x1, wq_bf, q_g, keys_bf, n_heads, d_key, n_keys):
    tp, d_model = x1.shape
    tb = SEL_TB
    row_spec = pl.BlockSpec((n_heads, n_keys, tb), lambda i: (0, 0, i))
    tile_spec = pl.BlockSpec((n_heads, n_keys // 2, tb), lambda i: (0, 0, i))
    out_shape = [jax.ShapeDtypeStruct((n_heads, n_keys, tp), F32)] * 2 + [
        jax.ShapeDtypeStruct((n_heads, n_keys // 2, tp), jnp.uint32)] * 2
    return pl.pallas_call(
        functools.partial(_peer_sel_kernel, n_heads=n_heads, d_key=d_key, topk=PEER_TOPK),
        grid=(tp // tb,),
        in_specs=[pl.BlockSpec((tb, d_model), lambda i: (i, 0)),
                  pl.BlockSpec(wq_bf.shape, lambda i: (0, 0)),
                  pl.BlockSpec((1, d_key), lambda i: (0, 0)),
                  pl.BlockSpec(keys_bf.shape, lambda i: (0, 0, 0))],
        out_specs=[row_spec, row_spec, tile_spec, tile_spec],
        out_shape=out_shape,
        compiler_params=_params("arbitrary"),
        name="peer_select",
    )(x1, wq_bf, q_g, keys_bf)


def _peer_dense_kernel(x_ref, u_ref, vt_ref, n1_ref, e1_ref, r2_ref, e2_ref,
                       g_ref, b_ref, op_ref, ot_ref, xt_ref, xtb_ref, acc_ref, act_ref, wg_ref,
                       *, n_heads, n_keys, lane_tile, alpha, n_prompt_blocks):
    c = pl.program_id(1)
    n_chunks = pl.num_programs(1) - 1
    n_i = act_ref.shape[0] // n_keys
    tb = act_ref.shape[1]
    zero = jnp.zeros((n_keys, lane_tile), BF16)

    @pl.when(c == 0)
    def _():
        acc_ref[...] = jnp.zeros(acc_ref.shape, F32)
        wg_ref[1] = jnp.zeros(wg_ref.shape[1:], BF16)
        xt = x_ref[...].T
        xt_ref[...] = xt
        xtb_ref[...] = xt.astype(BF16)

    def row_tile(ref, h, il, cols):
        row = jnp.broadcast_to(ref[h, il:il + 1, cols], (BF16_SUBLANES, lane_tile)).astype(BF16)
        return jnp.tile(row, (n_keys // BF16_SUBLANES, 1))

    def pv_previous():
        acc_ref[...] += jnp.dot(vt_ref[...], wg_ref[(c + 1) % 2], preferred_element_type=F32)

    @pl.when(c < n_chunks)
    def _():
        pv_previous()
        act_ref[...] = jnp.dot(u_ref[...], xtb_ref[...], preferred_element_type=F32)
        cur = c % 2
        for tc in range(tb // lane_tile):
            cols = slice(tc * lane_tile, (tc + 1) * lane_tile)
            for il in range(n_i):
                w = zero
                for h in range(n_heads):
                    sel = pltpu.bitcast(r2_ref[h, :, cols], BF16) < row_tile(n1_ref, h, il, cols)
                    e2 = pltpu.bitcast(e2_ref[h, :, cols], BF16)
                    w = w + jnp.where(sel, e2, zero) * row_tile(e1_ref, h, il, cols)
                rows = slice(il * n_keys, (il + 1) * n_keys)
                wg_ref[cur, rows, cols] = w * _gelu(act_ref[rows, cols]).astype(BF16)

    @pl.when(c == n_chunks)
    def _():
        pv_previous()
        y = _layer_norm(alpha * xt_ref[...] + acc_ref[...], g_ref[...], b_ref[...], 0).T
        t = pl.program_id(0)

        @pl.when(t < n_prompt_blocks)
        def _():
            op_ref[...] = y

        @pl.when(t >= n_prompt_blocks)
        def _():
            ot_ref[...] = y


def _peer_dense(x1, u_bf, vt_bf, n1, e1, r2, e2, ln_g, ln_b, alpha):
    tp, d_model = x1.shape
    n_heads, n_keys, _ = n1.shape
    n_exp = u_bf.shape[0]
    tb, ec = PEER_TB, PEER_EC
    n_i = ec // n_keys
    n_chunks = n_exp // ec
    n_prompt_blocks = tp // tb - 1
    tok = lambda t, c: (t, 0)
    cur = lambda c: jnp.minimum(c, n_chunks - 1)
    prev = lambda c: jnp.maximum(c - 1, 0)
    return pl.pallas_call(
        functools.partial(_peer_dense_kernel, n_heads=n_heads, n_keys=n_keys,
                          lane_tile=PEER_LANE_TILE, alpha=alpha, n_prompt_blocks=n_prompt_blocks),
        grid=(tp // tb, n_chunks + 1),
        in_specs=[pl.BlockSpec((tb, d_model), tok),
                  pl.BlockSpec((ec, d_model), lambda t, c: (cur(c), 0)),
                  pl.BlockSpec((d_model, ec), lambda t, c: (0, prev(c))),
                  pl.BlockSpec((n_heads, n_i, tb), lambda t, c: (0, cur(c), t)),
                  pl.BlockSpec((n_heads, n_i, tb), lambda t, c: (0, cur(c), t)),
                  pl.BlockSpec((n_heads, n_keys // 2, tb), lambda t, c: (0, 0, t)),
                  pl.BlockSpec((n_heads, n_keys // 2, tb), lambda t, c: (0, 0, t)),
                  pl.BlockSpec((d_model, 1), lambda t, c: (0, 0)),
                  pl.BlockSpec((d_model, 1), lambda t, c: (0, 0))],
        out_specs=[pl.BlockSpec((tb, d_model), lambda t, c: (jnp.minimum(t, n_prompt_blocks - 1), 0)),
                   pl.BlockSpec((tb, d_model), lambda t, c: (0, 0))],
        out_shape=[jax.ShapeDtypeStruct((n_prompt_blocks * tb, d_model), F32),
                   jax.ShapeDtypeStruct((tb, d_model), F32)],
        scratch_shapes=[pltpu.VMEM((d_model, tb), F32), pltpu.VMEM((d_model, tb), BF16),
                        pltpu.VMEM((d_model, tb), F32), pltpu.VMEM((ec, tb), F32),
                        pltpu.VMEM((2, ec, tb), BF16)],
        compiler_params=_params("arbitrary", "arbitrary"),
        name="peer_dense",
    )(x1, u_bf, vt_bf, n1, e1, r2, e2, ln_g, ln_b)


def kernel(x_prompt, x_sample, cache_k, cache_v, state_ssm_re, state_ssm_im, page_table,
           w_in, lambda_q1, lambda_k1, lambda_q2, lambda_k2, attn_subln_g,
           ssm_lambda_re, ssm_lambda_im, ssm_log_dt, ssm_b_re, ssm_b_im, ssm_c_re, ssm_c_im,
           ssm_d, ssm_w_glu, ssm_b_glu, w_out, ln1_g, ln1_b,
           peer_w_q, peer_q_g, peer_sub_keys, peer_u, peer_v, ln2_g, ln2_b):
    depth = w_in.shape[0]
    batch, seq, d_model = x_prompt.shape
    dec_batch, dec_seq, _ = x_sample.shape
    _, n_phys, page, n_heads, _, dqk = cache_k.shape
    dv = cache_v.shape[-1]
    d_attn = n_heads * dv
    groups, p_dim = state_ssm_re.shape[2:]
    ch = ssm_b_re.shape[-1]
    peer_heads, _, n_keys, d_half = peer_sub_keys.shape[1:]
    d_key = 2 * d_half
    alpha = (2.0 * depth) ** 0.25
    scale = dqk ** -0.5

    n_prompt = batch * seq
    n_sample = dec_batch * dec_seq
    n_tok = n_prompt + n_sample
    tp = -(-n_tok // TOKEN_PAD) * TOKEN_PAD
    n_chunks = seq // SSM_CHUNK
    n_steps = int(math.log2(n_chunks))
    assert n_chunks == 1 << n_steps and seq % ATTN_TQ == 0 and n_prompt % TOKEN_PAD == 0

    assert tp == n_prompt + TOKEN_PAD and TOKEN_PAD == PROJ_TM == PEER_TB
    pad_tail = lambda a: jnp.pad(a, ((0, TOKEN_PAD - n_sample), (0, 0)))
    x_p = x_prompt.reshape(n_prompt, d_model)
    x_t = pad_tail(x_sample.reshape(n_sample, d_model))
    slope_vals = 2.0 ** (-8.0 * jnp.arange(1, n_heads + 1, dtype=F32) / n_heads)
    slopes = jnp.broadcast_to(slope_vals[:, None, None], (n_heads, 1, 128))
    slopecol = jnp.repeat(slope_vals, 2 * dec_seq)[:, None]
    eye_hc = jnp.eye(2 * n_heads, dtype=F32)

    outs = {name: [] for name in ("kp", "vp", "hpr", "hpi", "ks", "vs", "hsr", "hsi")}
    for l in range(depth):
        lam_init = 0.8 - 0.6 * math.exp(-0.3 * l)
        lamv = jnp.stack([lambda_q1[l], lambda_k1[l], lambda_q2[l], lambda_k2[l]]).astype(F32)
        subln_g = attn_subln_g[l].astype(F32)[None, :]

        w_bf = w_in[l].astype(BF16)
        kt_prompt, k_last, v_prompt, v_last, u_all, qkv_bf = _project(
            x_p, x_t, w_bf, w_bf[:, d_attn:2 * d_attn].T, d_attn, n_heads, scale, batch, seq)

        a_prompt = _attn_prompt(qkv_bf, lamv, subln_g, slopes, batch, seq, n_heads, dv, dqk,
                                lam_init)

        qkv_s = qkv_bf[n_prompt:n_tok].reshape(dec_batch, dec_seq, 3 * d_attn)
        q_s = qkv_s[..., :d_attn].reshape(dec_batch, dec_seq, 2 * n_heads, dqk)
        qbd = (q_s.transpose(0, 2, 1, 3)[:, :, :, None, :].astype(F32)
               * eye_hc[None, :, None, :, None]).astype(BF16)
        qbd = qbd.reshape(dec_batch, 2 * n_heads * dec_seq, d_attn)
        k_new = jnp.pad(qkv_s[..., d_attn:2 * d_attn].transpose(0, 2, 1),
                        ((0, 0), (0, 0), (0, page - dec_seq)))
        v_new = jnp.pad(qkv_s[..., 2 * d_attn:], ((0, 0), (0, page - dec_seq), (0, 0)))
        kt2d = jnp.transpose(cache_k[l], (0, 2, 3, 4, 1)).reshape(n_phys * d_attn, page)
        a_s = _attn_sample(page_table, lamv, subln_g, slopecol, qbd, kt2d,
                           cache_v[l].reshape(n_phys * page * n_heads, dv),
                           k_new, v_new, n_heads, dec_seq, lam_init)
        a_sample = a_s.reshape(dec_batch, n_heads, dec_seq, dv).transpose(0, 2, 1, 3)
        a_tail = pad_tail(a_sample.reshape(n_sample, d_attn))

        ssm_p = (ssm_lambda_re[l], ssm_lambda_im[l], ssm_log_dt[l], ssm_b_re[l], ssm_b_im[l],
                 ssm_c_re[l], ssm_c_im[l], ssm_d[l])
        tm, sm, cm, dsk, coef = _ssm_tables(*ssm_p, SSM_CHUNK, n_steps)
        u_r = (u_all[:n_prompt].reshape(batch, n_chunks, SSM_CHUNK, groups, ch)
               .transpose(0, 3, 1, 2, 4).reshape(batch, groups, n_chunks, SSM_CHUNK * ch))
        y_r, h_p = _ssm_prompt(u_r, tm, sm, cm, dsk, coef)
        yg_prompt = (y_r.reshape(batch, groups, n_chunks, SSM_CHUNK, ch)
                     .transpose(0, 2, 3, 1, 4).reshape(n_prompt, groups * ch))

        tm_s, sm_s, cm_s, dsk_s, coef_s = _ssm_tables(*ssm_p, dec_seq, 1)
        us_r = (u_all[n_prompt:n_tok].reshape(dec_batch, dec_seq, groups, ch)
                .transpose(2, 0, 1, 3).reshape(groups, dec_batch, dec_seq * ch))
        h0r = state_ssm_re[l].astype(F32).transpose(1, 0, 2)
        h0i = state_ssm_im[l].astype(F32).transpose(1, 0, 2)
        ys_r, h_s = _ssm_sample(us_r, jnp.concatenate([h0r, h0i], -1),
                                jnp.concatenate([h0i, h0r], -1),
                                tm_s, sm_s[..., :2 * p_dim], cm_s, dsk_s, coef_s)
        yg_sample = (ys_r.reshape(groups, dec_batch, dec_seq, ch)
                     .transpose(1, 2, 0, 3).reshape(n_sample, groups * ch))
        x1 = _out_proj((x_p, x_t), (a_prompt, a_tail), (yg_prompt, pad_tail(yg_sample)),
                       ssm_w_glu[l].astype(BF16),
                       ssm_b_glu[l].astype(F32)[None, :], w_out[l].astype(BF16),
                       ln1_g[l].astype(F32)[None, :], ln1_b[l].astype(F32)[None, :], alpha)

        keys_bf = peer_sub_keys[l].reshape(2 * peer_heads, n_keys, d_half).astype(BF16)
        n1, e1, r2, e2 = _peer_select(x1, peer_w_q[l].astype(BF16),
                                      peer_q_g[l].astype(F32)[None, :], keys_bf,
                                      peer_heads, d_key, n_keys)
        x_p, x_t = _peer_dense(x1, peer_u[l].astype(BF16),
                            peer_v[l].T.astype(BF16), n1, e1, r2, e2,
                            ln2_g[l].astype(F32)[:, None], ln2_b[l].astype(F32)[:, None], alpha)

        outs["kp"].append(kt_prompt.reshape(batch, n_heads, 2, dqk, seq).transpose(0, 4, 1, 2, 3))
        outs["vp"].append(v_prompt.reshape(batch, seq, n_heads, dv))
        outs["hpr"].append(h_p[:, :, 0, :p_dim])
        outs["hpi"].append(h_p[:, :, 0, p_dim:])
        outs["ks"].append(k_last[:n_sample].reshape(dec_batch, dec_seq, n_heads, 2, dqk))
        outs["vs"].append(v_last[:n_sample * n_heads].reshape(dec_batch, dec_seq, n_heads, dv))
        outs["hsr"].append(h_s[:, :, :p_dim].transpose(1, 0, 2))
        outs["hsi"].append(h_s[:, :, p_dim:].transpose(1, 0, 2))

    y_prompt = x_p.reshape(batch, seq, d_model)
    y_sample = x_t[:n_sample].reshape(dec_batch, dec_seq, d_model)
    st = lambda name: jnp.stack(outs[name])
    return (y_prompt, y_sample, st("kp"), st("vp"), st("hpr"), st("hpi"),
            st("ks"), st("vs"), st("hsr"), st("hsi"))
```
